```python
import jax, jax.numpy as jnp
from jax import lax
import numpy as np

D_MODEL = 1024
BATCH = 2
SEQ = 8192
DEPTH = 2

N_HEADS = 8
HEAD_DIM = 64
ATTN_WIDTH = N_HEADS * HEAD_DIM
CONV_WIDTH = D_MODEL // 2
CONV_KERNEL = 31
MOBA_BLOCK = 256
MOBA_TOPK = 3
Q_CHUNK = 64
D_FF = 4 * D_MODEL
EPS = 1e-6
IN_COLS = 3 * ATTN_WIDTH + 2 * CONV_WIDTH + 2 * D_MODEL

kernel_name = "hybrid_moba_conformer_conv_block"


def rmsnorm(x, g):
    xf = x.astype(jnp.float32)
    y = xf * lax.rsqrt(jnp.mean(xf * xf, axis=-1, keepdims=True) + EPS)
    return (y * g.astype(jnp.float32)).astype(x.dtype)


def layernorm(x, g, b):
    xf = x.astype(jnp.float32)
    mu = jnp.mean(xf, axis=-1, keepdims=True)
    var = jnp.mean(jnp.square(xf - mu), axis=-1, keepdims=True)
    y = (xf - mu) * lax.rsqrt(var + EPS)
    return (y * g.astype(jnp.float32) + b.astype(jnp.float32)).astype(x.dtype)


def gather_blocks(blocks, idx):
    return jax.vmap(jax.vmap(lambda bl, ix: bl[ix]))(blocks, idx)


def moba_attention(q, k, v):
    b, h, s, dh = q.shape
    nb = -(-s // MOBA_BLOCK)
    s_pad = nb * MOBA_BLOCK
    pad = [(0, 0), (0, 0), (0, s_pad - s), (0, 0)]
    kb = jnp.pad(k, pad).reshape(b, h, nb, MOBA_BLOCK, dh)
    vb = jnp.pad(v, pad).reshape(b, h, nb, MOBA_BLOCK, dh)
    scale = dh ** -0.5
    k_mean = jnp.mean(kb.astype(jnp.float32), axis=3)
    gate = jnp.einsum('bhsd,bhnd->bhsn', q.astype(jnp.float32), k_mean)
    q_blk = jnp.arange(s) // MOBA_BLOCK
    past = jnp.arange(nb)[None, :] < q_blk[:, None]
    gate = jnp.where(past, gate, -jnp.inf)
    k_sel = max(1, min(MOBA_TOPK, nb - 1))
    top_val, top_idx = lax.top_k(gate, k_sel)
    sel_valid = jnp.isfinite(top_val)
    pos_in_block = jnp.arange(MOBA_BLOCK)
    n_chunks = s // Q_CHUNK

    def chunk(i):
        s0 = i * Q_CHUNK
        qc = lax.dynamic_slice_in_dim(q, s0, Q_CHUNK, axis=2)
        idx = lax.dynamic_slice_in_dim(top_idx, s0, Q_CHUNK, axis=2)
        ok = lax.dynamic_slice_in_dim(sel_valid, s0, Q_CHUNK, axis=2)
        k_g = gather_blocks(kb, idx)
        v_g = gather_blocks(vb, idx)
        own = s0 // MOBA_BLOCK
        k_own = lax.dynamic_index_in_dim(kb, own, axis=2, keepdims=False)
        v_own = lax.dynamic_index_in_dim(vb, own, axis=2, keepdims=False)
        l_sel = jnp.einsum('bhcd,bhckld->bhckl', qc, k_g).astype(jnp.float32) * scale
        l_sel = jnp.where(ok[..., None], l_sel, -jnp.inf)
        l_own = jnp.einsum('bhcd,bhld->bhcl', qc, k_own).astype(jnp.float32) * scale
        qpos = s0 + jnp.arange(Q_CHUNK)
        causal = (own * MOBA_BLOCK + pos_in_block)[None, :] <= qpos[:, None]
        l_own = jnp.where(causal, l_own, -jnp.inf)
        logits = jnp.concatenate([l_sel.reshape(b, h, Q_CHUNK, k_sel * MOBA_BLOCK), l_own], axis=-1)
        p = jax.nn.softmax(logits, axis=-1)
        p_sel = p[..., :k_sel * MOBA_BLOCK].reshape(b, h, Q_CHUNK, k_sel, MOBA_BLOCK)
        p_own = p[..., k_sel * MOBA_BLOCK:]
        o = (jnp.einsum('bhckl,bhckld->bhcd', p_sel, v_g.astype(jnp.float32))
             + jnp.einsum('bhcl,bhld->bhcd', p_own, v_own.astype(jnp.float32)))
        return o.astype(q.dtype)

    out = lax.map(chunk, jnp.arange(n_chunks))
    return out.transpose(1, 2, 0, 3, 4).reshape(b, h, s, dh)


def conformer_conv(u, w_dw, b_dw, ln_g, ln_b):
    a, g = jnp.split(u, 2, axis=-1)
    z = a * jax.nn.sigmoid(g)
    z = lax.conv_general_dilated(
        z, w_dw[:, None, :].astype(z.dtype), window_strides=(1,),
        padding=[(CONV_KERNEL - 1, 0)],
        dimension_numbers=('NWC', 'WIO', 'NWC'),
        feature_group_count=CONV_WIDTH) + b_dw
    z = layernorm(z, ln_g, ln_b)
    return jax.nn.silu(z)


def setup_inputs(seed: int = 0) -> dict:
    key = jax.random.key(seed)
    ks = jax.random.split(key, 20)
    n = jax.random.normal
    f = jnp.float32
    return {
        "x": n(ks[0], (BATCH, SEQ, D_MODEL), f),
        "g_mix": 1.0 + 0.02 * n(ks[1], (DEPTH, D_MODEL), f),
        "w_in": n(ks[2], (DEPTH, D_MODEL, IN_COLS), f) * D_MODEL ** -0.5,
        "b_glu": 0.02 * n(ks[3], (DEPTH, 2 * CONV_WIDTH), f),
        "w_dw": n(ks[4], (DEPTH, CONV_KERNEL, CONV_WIDTH), f) * CONV_KERNEL ** -0.5,
        "b_dw": 0.02 * n(ks[5], (DEPTH, CONV_WIDTH), f),
        "ln_g": 1.0 + 0.02 * n(ks[6], (DEPTH, CONV_WIDTH), f),
        "ln_b": 0.02 * n(ks[7], (DEPTH, CONV_WIDTH), f),
        "w_conv_pw": n(ks[8], (DEPTH, CONV_WIDTH, D_MODEL), f) * CONV_WIDTH ** -0.5,
        "b_conv_pw": 0.02 * n(ks[9], (DEPTH, D_MODEL), f),
        "w_attn_up": n(ks[10], (DEPTH, ATTN_WIDTH, D_MODEL), f) * ATTN_WIDTH ** -0.5,
        "w_out": n(ks[11], (DEPTH, D_MODEL, D_MODEL), f) * D_MODEL ** -0.5,
        "g_ffn": 1.0 + 0.02 * n(ks[12], (DEPTH, D_MODEL), f),
        "w_ff1": n(ks[13], (DEPTH, D_MODEL, D_FF), f) * D_MODEL ** -0.5,
        "w_ff2": n(ks[14], (DEPTH, D_FF, D_MODEL), f) * D_FF ** -0.5,
        "g_final": 1.0 + 0.02 * n(ks[15], (D_MODEL,), f),
    }


def reference(x, g_mix, w_in, b_glu, w_dw, b_dw, ln_g, ln_b, w_conv_pw, b_conv_pw,
              w_attn_up, w_out, g_ffn, w_ff1, w_ff2, g_final):
    b, s, _ = x.shape
    c_q = ATTN_WIDTH
    c_k = 2 * ATTN_WIDTH
    c_v = 3 * ATTN_WIDTH
    c_u = c_v + 2 * CONV_WIDTH
    c_ga = c_u + D_MODEL
    for l in range(DEPTH):
        h = rmsnorm(x, g_mix[l])
        proj = h @ w_in[l]
        q = proj[..., :c_q].reshape(b, s, N_HEADS, HEAD_DIM).transpose(0, 2, 1, 3)
        k = proj[..., c_q:c_k].reshape(b, s, N_HEADS, HEAD_DIM).transpose(0, 2, 1, 3)
        v = proj[..., c_k:c_v].reshape(b, s, N_HEADS, HEAD_DIM).transpose(0, 2, 1, 3)
        u = proj[..., c_v:c_u] + b_glu[l]
        gate_a = jax.nn.sigmoid(proj[..., c_u:c_ga])
        gate_c = jax.nn.sigmoid(proj[..., c_ga:])
        o = moba_attention(q, k, v).transpose(0, 2, 1, 3).reshape(b, s, ATTN_WIDTH)
        branch_a = o @ w_attn_up[l]
        cz = conformer_conv(u, w_dw[l], b_dw[l], ln_g[l], ln_b[l])
        branch_c = cz @ w_conv_pw[l] + b_conv_pw[l]
        merged = gate_a * branch_a + gate_c * branch_c
        x = x + merged @ w_out[l]
        h = rmsnorm(x, g_ffn[l])
        x = x + jnp.square(jax.nn.relu(h @ w_ff1[l])) @ w_ff2[l]
    return rmsnorm(x, g_final)
```

```python
import functools

import jax
import jax.numpy as jnp
from jax import lax
from jax.experimental import pallas as pl
from jax.experimental.pallas import tpu as pltpu

D_MODEL = 1024
N_HEADS = 8
HEAD_DIM = 64
ATTN_WIDTH = N_HEADS * HEAD_DIM
CONV_WIDTH = D_MODEL // 2
CONV_KERNEL = 31
MOBA_BLOCK = 256
MOBA_TOPK = 3
D_FF = 4 * D_MODEL
EPS = 1e-6
IN_COLS = 3 * ATTN_WIDTH + 2 * CONV_WIDTH + 2 * D_MODEL

LANES = 128
HEADS_PER_TILE = LANES // HEAD_DIM
N_HEAD_TILES = ATTN_WIDTH // LANES
HALO = 32
CONV_ROWS = 32
TM_INPROJ = 512
TM_CONV = 256
TM_FFN = 256
VMEM_LIMIT = 56 * 1024 * 1024

F32 = jnp.float32
BF16 = jnp.bfloat16
NEG_INF = float("-inf")


def _const_spec(shape):
    return pl.BlockSpec(shape, lambda *_: (0,) * len(shape), pipeline_mode=pl.Buffered(1))


def _inproj_kernel(x_ref, g_ref, w_ref, bglu_ref,
                   q_ref, k_ref, v_ref, z_ref, ga_ref, gc_ref, ksum_ref):
    x = x_ref[...]
    ms = jnp.mean(x * x, axis=-1, keepdims=True)
    h = ((x * lax.rsqrt(ms + EPS)) * g_ref[...]).astype(BF16)

    def proj(lo, hi):
        return jnp.dot(h, w_ref[:, lo:hi], preferred_element_type=F32)

    c_q, c_k, c_v = ATTN_WIDTH, 2 * ATTN_WIDTH, 3 * ATTN_WIDTH
    c_a = c_v + CONV_WIDTH
    c_u = c_v + 2 * CONV_WIDTH
    c_ga = c_u + D_MODEL

    q_ref[...] = proj(0, c_q).astype(BF16)
    k = proj(c_q, c_k)
    k_ref[...] = k.astype(BF16)
    for j in range(k.shape[0] // MOBA_BLOCK):
        ksum_ref[j] = jnp.sum(k[j * MOBA_BLOCK:(j + 1) * MOBA_BLOCK], axis=0, keepdims=True)
    v_ref[...] = proj(c_k, c_v).astype(BF16)
    a = proj(c_v, c_a) + bglu_ref[:, :CONV_WIDTH]
    gg = proj(c_a, c_u) + bglu_ref[:, CONV_WIDTH:]
    z_ref[...] = a * jax.nn.sigmoid(gg)
    ga_ref[...] = jax.nn.sigmoid(proj(c_u, c_ga)).astype(BF16)
    gc_ref[...] = jax.nn.sigmoid(proj(c_ga, IN_COLS)).astype(BF16)


def _inproj(x, g, w, bglu):
    t = x.shape[0]
    tm = TM_INPROJ
    row = lambda w_: pl.BlockSpec((tm, w_), lambda i: (i, 0))
    return pl.pallas_call(
        _inproj_kernel,
        grid=(t // tm,),
        in_specs=[row(D_MODEL), _const_spec((1, D_MODEL)), _const_spec((D_MODEL, IN_COLS)),
                  _const_spec((1, 2 * CONV_WIDTH))],
        out_specs=[row(ATTN_WIDTH), row(ATTN_WIDTH), row(ATTN_WIDTH), row(CONV_WIDTH),
                   row(D_MODEL), row(D_MODEL),
                   pl.BlockSpec((tm // MOBA_BLOCK, 1, ATTN_WIDTH), lambda i: (i, 0, 0))],
        out_shape=[jax.ShapeDtypeStruct((t, ATTN_WIDTH), BF16)] * 3
        + [jax.ShapeDtypeStruct((t, CONV_WIDTH), F32)]
        + [jax.ShapeDtypeStruct((t, D_MODEL), BF16)] * 2
        + [jax.ShapeDtypeStruct((t // MOBA_BLOCK, 1, ATTN_WIDTH), F32)],
        compiler_params=pltpu.CompilerParams(dimension_semantics=("arbitrary",),
                                             vmem_limit_bytes=VMEM_LIMIT),
        name="inproj",
    )(x, g, w, bglu)


def _attn_kernel(q_ref, k_ref, vt_ref, ksum_ref, o_ref, sel_ref, *, n_blocks):
    blk_len = MOBA_BLOCK
    qb = pl.program_id(2)
    q = q_ref[0]
    lane = lax.broadcasted_iota(jnp.int32, q.shape, 1)
    kmean = ksum_ref[0] * (1.0 / blk_len)
    km_hi = kmean.astype(BF16)
    r1 = kmean - km_hi.astype(F32)
    km_mid = r1.astype(BF16)
    km_lo = (r1 - km_mid.astype(F32)).astype(BF16)
    km3 = jnp.concatenate([km_hi, km_mid, km_lo], axis=0)

    blk = lax.broadcasted_iota(jnp.int32, (n_blocks, blk_len), 0)
    kpos = lax.broadcasted_iota(jnp.int32, (blk_len, blk_len), 0)
    qpos = lax.broadcasted_iota(jnp.int32, (blk_len, blk_len), 1)
    nt = (((1,), (1,)), ((), ()))
    scale = HEAD_DIM ** -0.5

    outs = []
    for hh in range(HEADS_PER_TILE):
        in_head = (lane >= HEAD_DIM * hh) & (lane < HEAD_DIM * (hh + 1))
        qh = jnp.where(in_head, q, jnp.zeros_like(q))
        g3 = lax.dot_general(km3, qh, nt, preferred_element_type=F32)
        gate = g3[0:n_blocks] + g3[n_blocks:2 * n_blocks] + g3[2 * n_blocks:3 * n_blocks]
        g = jnp.where(blk < qb, gate, NEG_INF)
        sel = jnp.zeros((n_blocks, blk_len), F32)
        for _ in range(MOBA_TOPK):
            mx = jnp.max(g, axis=0, keepdims=True)
            idx = jnp.min(jnp.where(g == mx, blk, n_blocks), axis=0, keepdims=True)
            pick = blk == idx
            sel = jnp.where(pick & jnp.isfinite(mx), 1.0, sel)
            g = jnp.where(pick, NEG_INF, g)
        for j in range(n_blocks):
            sel_ref[hh, j] = sel[j:j + 1, :]

        qs = qh * scale
        rows = slice(HEAD_DIM * hh, HEAD_DIM * (hh + 1))
        s = lax.dot_general(k_ref[0, 0, qb], qs, nt, preferred_element_type=F32)
        s = jnp.where(kpos <= qpos, s, NEG_INF)
        m = jnp.max(s, axis=0, keepdims=True)
        p = jnp.exp(s - m)
        l = jnp.sum(p, axis=0, keepdims=True)
        acc = jnp.dot(vt_ref[0, 0, qb, rows, :], p.astype(BF16), preferred_element_type=F32)

        def body(j, carry):
            m, l, acc = carry
            s = lax.dot_general(k_ref[0, 0, j], qs, nt, preferred_element_type=F32)
            s = jnp.where(sel_ref[hh, j] > 0.0, s, NEG_INF)
            m_new = jnp.maximum(m, jnp.max(s, axis=0, keepdims=True))
            alpha = jnp.exp(m - m_new)
            p = jnp.exp(s - m_new)
            l = alpha * l + jnp.sum(p, axis=0, keepdims=True)
            acc = alpha * acc + jnp.dot(vt_ref[0, 0, j, rows, :], p.astype(BF16),
                                        preferred_element_type=F32)
            return m_new, l, acc

        m, l, acc = lax.fori_loop(0, qb, body, (m, l, acc))
        outs.append(acc / l)
    o_ref[0] = jnp.concatenate(outs, axis=0).T.astype(BF16)


def _attention(q, k5, vt5, ksum):
    b, s, _ = q.shape
    nb = s // MOBA_BLOCK
    blk = MOBA_BLOCK
    return pl.pallas_call(
        functools.partial(_attn_kernel, n_blocks=nb),
        grid=(b, N_HEAD_TILES, nb),
        in_specs=[
            pl.BlockSpec((1, blk, LANES), lambda bi, hp, qi: (bi, qi, hp)),
            pl.BlockSpec((1, 1, nb, blk, LANES), lambda bi, hp, qi: (bi, hp, 0, 0, 0)),
            pl.BlockSpec((1, 1, nb, LANES, blk), lambda bi, hp, qi: (bi, hp, 0, 0, 0)),
            pl.BlockSpec((1, nb, LANES), lambda bi, hp, qi: (bi, 0, hp)),
        ],
        out_specs=pl.BlockSpec((1, blk, LANES), lambda bi, hp, qi: (bi, qi, hp)),
        out_shape=jax.ShapeDtypeStruct((b, s, ATTN_WIDTH), BF16),
        scratch_shapes=[pltpu.VMEM((HEADS_PER_TILE, nb, 1, blk), F32)],
        compiler_params=pltpu.CompilerParams(
            dimension_semantics=("arbitrary", "arbitrary", "arbitrary"),
            vmem_limit_bytes=VMEM_LIMIT),
        name="moba_attn",
    )(q, k5, vt5, ksum)


def _conv_kernel(z_ref, halo_ref, w_ref, bdw_ref, lng_ref, lnb_ref, cz_ref, zbuf, *, tiles_per_seq):
    tm = z_ref.shape[0]
    first = (pl.program_id(0) % tiles_per_seq) == 0
    zbuf[0:HALO, :] = jnp.where(first, 0.0, halo_ref[...])
    zbuf[HALO:HALO + tm, :] = z_ref[...]
    lead = HALO - (CONV_KERNEL - 1)
    for r in range(tm // CONV_ROWS):
        base = r * CONV_ROWS + lead
        acc = jnp.zeros((CONV_ROWS, CONV_WIDTH), F32) + bdw_ref[...]
        for t in range(CONV_KERNEL):
            acc = acc + w_ref[t:t + 1, :] * zbuf[base + t:base + t + CONV_ROWS, :]
        mu = jnp.mean(acc, axis=-1, keepdims=True)
        d = acc - mu
        var = jnp.mean(d * d, axis=-1, keepdims=True)
        y = (d * lax.rsqrt(var + EPS)) * lng_ref[...] + lnb_ref[...]
        cz_ref[r * CONV_ROWS:(r + 1) * CONV_ROWS, :] = (y * jax.nn.sigmoid(y)).astype(BF16)


def _conv_branch(z, w_dw, b_dw, ln_g, ln_b, seq):
    t = z.shape[0]
    tm = TM_CONV
    halo_blocks = tm // HALO
    return pl.pallas_call(
        functools.partial(_conv_kernel, tiles_per_seq=seq // tm),
        grid=(t // tm,),
        in_specs=[
            pl.BlockSpec((tm, CONV_WIDTH), lambda i: (i, 0)),
            pl.BlockSpec((HALO, CONV_WIDTH), lambda i: (jnp.maximum(i * halo_blocks - 1, 0), 0)),
            _const_spec((CONV_KERNEL, CONV_WIDTH)), _const_spec((1, CONV_WIDTH)),
            _const_spec((1, CONV_WIDTH)), _const_spec((1, CONV_WIDTH)),
        ],
        out_specs=pl.BlockSpec((tm, CONV_WIDTH), lambda i: (i, 0)),
        out_shape=jax.ShapeDtypeStruct((t, CONV_WIDTH), BF16),
        scratch_shapes=[pltpu.VMEM((HALO + tm, CONV_WIDTH), F32)],
        compiler_params=pltpu.CompilerParams(dimension_semantics=("arbitrary",),
                                             vmem_limit_bytes=VMEM_LIMIT),
        name="conv_branch",
    )(z, z, w_dw, b_dw, ln_g, ln_b)


def _mix_ffn_kernel(o_ref, cz_ref, ga_ref, gc_ref, x_ref, wup_ref, wpw_ref, bpw_ref, wout_ref,
                    gffn_ref, w1_ref, w2_ref, gfin_ref, out_ref, t_ref, *, final_norm):
    def rms(v, g_ref):
        ms = jnp.mean(v * v, axis=-1, keepdims=True)
        return (v * lax.rsqrt(ms + EPS)) * g_ref[...]

    branch_a = jnp.dot(o_ref[...], wup_ref[...], preferred_element_type=F32)
    branch_c = jnp.dot(cz_ref[...], wpw_ref[...], preferred_element_type=F32) + bpw_ref[...]
    merged = ga_ref[...].astype(F32) * branch_a + gc_ref[...].astype(F32) * branch_c
    x1 = x_ref[...] + jnp.dot(merged.astype(BF16), wout_ref[...], preferred_element_type=F32)
    h = rms(x1, gffn_ref).astype(BF16)
    chunk = D_MODEL
    for c in range(D_FF // chunk):
        u = jnp.dot(h, w1_ref[:, c * chunk:(c + 1) * chunk], preferred_element_type=F32)
        t_ref[:, c * chunk:(c + 1) * chunk] = jnp.square(jnp.maximum(u, 0.0)).astype(BF16)
    x2 = x1 + jnp.dot(t_ref[...], w2_ref[...], preferred_element_type=F32)
    out_ref[...] = rms(x2, gfin_ref) if final_norm else x2


def _mix_ffn(o, cz, ga, gc, x, wup, wpw, bpw, wout, gffn, w1, w2, gfin, final_norm):
    t = x.shape[0]
    tm = TM_FFN
    row = lambda w_: pl.BlockSpec((tm, w_), lambda i: (i, 0))
    return pl.pallas_call(
        functools.partial(_mix_ffn_kernel, final_norm=final_norm),
        grid=(t // tm,),
        in_specs=[row(ATTN_WIDTH), row(CONV_WIDTH), row(D_MODEL), row(D_MODEL), row(D_MODEL),
                  _const_spec((ATTN_WIDTH, D_MODEL)), _const_spec((CONV_WIDTH, D_MODEL)),
                  _const_spec((1, D_MODEL)), _const_spec((D_MODEL, D_MODEL)),
                  _const_spec((1, D_MODEL)), _const_spec((D_MODEL, D_FF)),
                  _const_spec((D_FF, D_MODEL)), _const_spec((1, D_MODEL))],
        out_specs=row(D_MODEL),
        out_shape=jax.ShapeDtypeStruct((t, D_MODEL), F32),
        scratch_shapes=[pltpu.VMEM((tm, D_FF), BF16)],
        compiler_params=pltpu.CompilerParams(dimension_semantics=("arbitrary",),
                                             vmem_limit_bytes=VMEM_LIMIT),
        name="mix_ffn",
    )(o, cz, ga, gc, x, wup, wpw, bpw, wout, gffn, w1, w2, gfin)


def kernel(x, g_mix, w_in, b_glu, w_dw, b_dw, ln_g, ln_b, w_conv_pw, b_conv_pw,
           w_attn_up, w_out, g_ffn, w_ff1, w_ff2, g_final):
    b, s, d = x.shape
    depth = w_in.shape[0]
    nb = s // MOBA_BLOCK
    assert d == D_MODEL and s % MOBA_BLOCK == 0 and (b * s) % TM_INPROJ == 0
    xf = x.reshape(b * s, d)
    row = lambda a: a.reshape(1, -1)
    for l in range(depth):
        q, k, v, z, ga, gc, ksum = _inproj(xf, row(g_mix[l]), w_in[l].astype(BF16), row(b_glu[l]))
        k5 = k.reshape(b, nb, MOBA_BLOCK, N_HEAD_TILES, LANES).transpose(0, 3, 1, 2, 4)
        vt5 = v.reshape(b, nb, MOBA_BLOCK, N_HEAD_TILES, LANES).transpose(0, 3, 1, 4, 2)
        o = _attention(q.reshape(b, s, ATTN_WIDTH), k5, vt5, ksum.reshape(b, nb, ATTN_WIDTH))
        cz = _conv_branch(z, w_dw[l], row(b_dw[l]), row(ln_g[l]), row(ln_b[l]), s)
        xf = _mix_ffn(o.reshape(b * s, ATTN_WIDTH), cz, ga, gc, xf,
                      w_attn_up[l].astype(BF16), w_conv_pw[l].astype(BF16), row(b_conv_pw[l]),
                      w_out[l].astype(BF16), row(g_ffn[l]), w_ff1[l].astype(BF16),
                      w_ff2[l].astype(BF16), row(g_final), final_norm=(l == depth - 1))
    return xf.reshape(b, s, d)
```

```python
import functools
import math

import jax
import jax.numpy as jnp
from jax import lax
from jax.experimental import pallas as pl
from jax.experimental.pallas import tpu as pltpu

D_MODEL = 1024
N_HEADS = 8
HEAD_DIM = 64
ATTN_WIDTH = N_HEADS * HEAD_DIM
CONV_WIDTH = D_MODEL // 2
CONV_KERNEL = 31
MOBA_BLOCK = 256
MOBA_TOPK = 3
D_FF = 4 * D_MODEL
EPS = 1e-6
IN_COLS = 3 * ATTN_WIDTH + 2 * CONV_WIDTH + 2 * D_MODEL

LANES = 128
PAIR = LANES // HEAD_DIM
N_PAIRS = N_HEADS // PAIR
K_LANES = 2 * LANES
V_ROWS = HEAD_DIM + 16
Q_GROUP = 4
MASK_BIAS = -(2.0 ** 100)
M_INIT = -(2.0 ** 99)
Q_SCALE = HEAD_DIM ** -0.5 * math.log2(math.e)
GATE_CHUNK = 8
HALO = 32
CONV_ROWS = 32
TM_INPROJ = 512
TM_CONV = 256
TM_FFN = 256
VMEM_LIMIT = 56 * 1024 * 1024

F32 = jnp.float32
BF16 = jnp.bfloat16
NEG_INF = float("-inf")


def _const_spec(shape):
    return pl.BlockSpec(shape, lambda *_: (0,) * len(shape), pipeline_mode=pl.Buffered(1))


def _inproj_kernel(x_ref, g_ref, w_ref, bglu_ref,
                   q_ref, k_ref, vt_ref, z_ref, ga_ref, gc_ref, ksum_ref):
    x = x_ref[...]
    tm = x.shape[0]
    blocks = tm // MOBA_BLOCK
    ms = jnp.mean(x * x, axis=-1, keepdims=True)
    h = ((x * lax.rsqrt(ms + EPS)) * g_ref[...]).astype(BF16)

    def proj(lo, hi):
        return jnp.dot(h, w_ref[:, lo:hi], preferred_element_type=F32)

    c_q, c_k, c_v = ATTN_WIDTH, 2 * ATTN_WIDTH, 3 * ATTN_WIDTH
    c_a = c_v + CONV_WIDTH
    c_u = c_v + 2 * CONV_WIDTH
    c_ga = c_u + D_MODEL

    q_ref[...] = (proj(0, c_q) * Q_SCALE).astype(BF16)

    k = proj(c_q, c_k)
    sel_lane = lax.broadcasted_iota(jnp.int32, (MOBA_BLOCK, LANES), 1)
    for j in range(blocks):
        kb = k[j * MOBA_BLOCK:(j + 1) * MOBA_BLOCK]
        ksum_ref[0, j] = jnp.sum(kb, axis=0, keepdims=True)
        onehot = (sel_lane == pl.program_id(1) * blocks + j).astype(BF16)
        for p in range(N_PAIRS):
            k_ref[0, p, j, :, 0:LANES] = kb[:, p * LANES:(p + 1) * LANES].astype(BF16)
            k_ref[0, p, j, :, LANES:K_LANES] = onehot

    vt = proj(c_k, c_v).T
    ones_row = (lax.broadcasted_iota(jnp.int32, (V_ROWS - HEAD_DIM, MOBA_BLOCK), 0) == 0).astype(BF16)
    for hd in range(N_HEADS):
        for j in range(blocks):
            vt_ref[0, hd, j, 0:HEAD_DIM, :] = vt[hd * HEAD_DIM:(hd + 1) * HEAD_DIM,
                                                 j * MOBA_BLOCK:(j + 1) * MOBA_BLOCK].astype(BF16)
            vt_ref[0, hd, j, HEAD_DIM:V_ROWS, :] = ones_row

    a = proj(c_v, c_a) + bglu_ref[:, :CONV_WIDTH]
    gg = proj(c_a, c_u) + bglu_ref[:, CONV_WIDTH:]
    z_ref[...] = a * jax.nn.sigmoid(gg)
    ga_ref[...] = jax.nn.sigmoid(proj(c_u, c_ga)).astype(BF16)
    gc_ref[...] = jax.nn.sigmoid(proj(c_ga, IN_COLS)).astype(BF16)


def _inproj(x, g, w, bglu, batch, seq):
    t = x.shape[0]
    tm = TM_INPROJ
    nt = seq // tm
    nb = seq // MOBA_BLOCK
    bpt = tm // MOBA_BLOCK
    row = lambda w_: pl.BlockSpec((tm, w_), lambda b, i: (b * nt + i, 0))
    return pl.pallas_call(
        _inproj_kernel,
        grid=(batch, nt),
        in_specs=[row(D_MODEL), _const_spec((1, D_MODEL)), _const_spec((D_MODEL, IN_COLS)),
                  _const_spec((1, 2 * CONV_WIDTH))],
        out_specs=[
            row(ATTN_WIDTH),
            pl.BlockSpec((1, N_PAIRS, bpt, MOBA_BLOCK, K_LANES), lambda b, i: (b, 0, i, 0, 0)),
            pl.BlockSpec((1, N_HEADS, bpt, V_ROWS, MOBA_BLOCK), lambda b, i: (b, 0, i, 0, 0)),
            row(CONV_WIDTH), row(D_MODEL), row(D_MODEL),
            pl.BlockSpec((1, bpt, 1, ATTN_WIDTH), lambda b, i: (b, i, 0, 0)),
        ],
        out_shape=[
            jax.ShapeDtypeStruct((t, ATTN_WIDTH), BF16),
            jax.ShapeDtypeStruct((batch, N_PAIRS, nb, MOBA_BLOCK, K_LANES), BF16),
            jax.ShapeDtypeStruct((batch, N_HEADS, nb, V_ROWS, MOBA_BLOCK), BF16),
            jax.ShapeDtypeStruct((t, CONV_WIDTH), F32),
            jax.ShapeDtypeStruct((t, D_MODEL), BF16),
            jax.ShapeDtypeStruct((t, D_MODEL), BF16),
            jax.ShapeDtypeStruct((batch, nb, 1, ATTN_WIDTH), F32),
        ],
        compiler_params=pltpu.CompilerParams(dimension_semantics=("arbitrary", "arbitrary"),
                                             vmem_limit_bytes=VMEM_LIMIT),
        name="inproj",
    )(x, g, w, bglu)


def _attn_kernel(q_ref, k_ref, vt_ref, ksum_ref, o_ref, qa_ref, m_ref, acc_ref, *, n_blocks):
    blk_len = MOBA_BLOCK
    nt = (((1,), (1,)), ((), ()))
    n_pad = qa_ref.shape[0]

    kmean = ksum_ref[0] * (1.0 / blk_len)
    km_hi = kmean.astype(BF16)
    r1 = kmean - km_hi.astype(F32)
    km_mid = r1.astype(BF16)
    km_lo = (r1 - km_mid.astype(F32)).astype(BF16)
    km3 = jnp.concatenate([km_hi, km_mid, km_lo], axis=0)
    width = GATE_CHUNK * blk_len
    lane = lax.broadcasted_iota(jnp.int32, (width, LANES), 1)
    blk = lax.broadcasted_iota(jnp.int32, (n_blocks, width), 0)
    col_blk = lax.broadcasted_iota(jnp.int32, (n_blocks, width), 1) // blk_len
    pad_rows = jnp.zeros((LANES - n_blocks, blk_len), F32)

    def gate_chunk(c, carry):
        qc = q_ref[0, pl.ds(pl.multiple_of(c * width, width), width), :]
        q_blk = col_blk + c * GATE_CHUNK
        for hd in range(PAIR):
            in_head = (lane >= HEAD_DIM * hd) & (lane < HEAD_DIM * (hd + 1))
            qh = jnp.where(in_head, qc, jnp.zeros_like(qc))
            g3 = lax.dot_general(km3, qh, nt, preferred_element_type=F32)
            gate = g3[0:n_blocks] + g3[n_blocks:2 * n_blocks] + g3[2 * n_blocks:3 * n_blocks]
            g = jnp.where(blk < q_blk, gate, NEG_INF)
            keep = blk == q_blk
            for _ in range(MOBA_TOPK):
                mx = jnp.max(g, axis=0, keepdims=True)
                idx = jnp.min(jnp.where(g == mx, blk, n_blocks), axis=0, keepdims=True)
                pick = blk == idx
                keep = keep | (pick & jnp.isfinite(mx))
                g = jnp.where(pick, NEG_INF, g)
            bias_t = jnp.where(keep, 0.0, MASK_BIAS)
            for t in range(GATE_CHUNK):
                cols = slice(t * blk_len, (t + 1) * blk_len)
                aug_t = jnp.concatenate([bias_t[:, cols], pad_rows], axis=0)
                qa_ref[c * GATE_CHUNK + t, hd, :, 0:LANES] = qh[cols]
                qa_ref[c * GATE_CHUNK + t, hd, :, LANES:K_LANES] = aug_t.T.astype(BF16)
        return carry

    lax.fori_loop(0, n_blocks // GATE_CHUNK, gate_chunk, 0)

    sel_lane = lax.broadcasted_iota(jnp.int32, (blk_len, LANES), 1)
    never = jnp.where(sel_lane < n_blocks, MASK_BIAS, 0.0).astype(BF16)
    for t in range(n_blocks, n_pad):
        for hd in range(PAIR):
            qa_ref[t, hd, :, 0:LANES] = jnp.zeros((blk_len, LANES), BF16)
            qa_ref[t, hd, :, LANES:K_LANES] = never
    m_ref[...] = jnp.full(m_ref.shape, M_INIT, F32)
    acc_ref[...] = jnp.zeros(acc_ref.shape, F32)

    kpos = lax.broadcasted_iota(jnp.int32, (blk_len, blk_len), 0)
    qpos = lax.broadcasted_iota(jnp.int32, (blk_len, blk_len), 1)

    def group(j, qb0, diagonal):
        qa = qa_ref[pl.ds(qb0, Q_GROUP)].reshape(Q_GROUP * PAIR * blk_len, K_LANES)
        s_wide = lax.dot_general(k_ref[0, 0, j], qa, nt, preferred_element_type=F32)
        for t in range(Q_GROUP):
            for hd in range(PAIR):
                col = (t * PAIR + hd) * blk_len
                s = s_wide[:, col:col + blk_len]
                if diagonal and t == 0:
                    s = jnp.where(kpos <= qpos, s, NEG_INF)
                m_old = m_ref[qb0 + t, hd]
                m_new = jnp.maximum(m_old, jnp.max(s, axis=0, keepdims=True))
                alpha = jnp.exp2(m_old - m_new)
                p = jnp.exp2(s - m_new).astype(BF16)
                acc_ref[qb0 + t, hd] = alpha * acc_ref[qb0 + t, hd] + jnp.dot(
                    vt_ref[0, hd, j], p, preferred_element_type=F32)
                m_ref[qb0 + t, hd] = m_new

    def key_block(j, carry):
        group(j, j, True)
        n_groups = (n_blocks - j + Q_GROUP - 1) // Q_GROUP

        def rest(gi, c):
            group(j, j + gi * Q_GROUP, False)
            return c

        lax.fori_loop(1, n_groups, rest, 0)
        return carry

    lax.fori_loop(0, n_blocks, key_block, 0)

    def finalize(t, carry):
        pair = []
        for hd in range(PAIR):
            acc = acc_ref[t, hd]
            pair.append(acc[0:HEAD_DIM] / acc[HEAD_DIM:HEAD_DIM + 1])
        rows = pl.ds(pl.multiple_of(t * blk_len, blk_len), blk_len)
        o_ref[0, rows, :] = jnp.concatenate(pair, axis=0).T.astype(BF16)
        return carry

    lax.fori_loop(0, n_blocks, finalize, 0)


def _attention(q, k, vt, ksum):
    b, s, _ = q.shape
    nb = s // MOBA_BLOCK
    blk = MOBA_BLOCK
    n_pad = nb + Q_GROUP - 1
    assert nb <= LANES and nb % GATE_CHUNK == 0
    return pl.pallas_call(
        functools.partial(_attn_kernel, n_blocks=nb),
        grid=(b, N_PAIRS),
        in_specs=[
            pl.BlockSpec((1, s, LANES), lambda bi, p: (bi, 0, p)),
            pl.BlockSpec((1, 1, nb, blk, K_LANES), lambda bi, p: (bi, p, 0, 0, 0)),
            pl.BlockSpec((1, PAIR, nb, V_ROWS, blk), lambda bi, p: (bi, p, 0, 0, 0)),
            pl.BlockSpec((1, nb, LANES), lambda bi, p: (bi, 0, p)),
        ],
        out_specs=pl.BlockSpec((1, s, LANES), lambda bi, p: (bi, 0, p)),
        out_shape=jax.ShapeDtypeStruct((b, s, ATTN_WIDTH), BF16),
        scratch_shapes=[pltpu.VMEM((n_pad, PAIR, blk, K_LANES), BF16),
                        pltpu.VMEM((n_pad, PAIR, 1, blk), F32),
                        pltpu.VMEM((n_pad, PAIR, V_ROWS, blk), F32)],
        compiler_params=pltpu.CompilerParams(dimension_semantics=("arbitrary", "arbitrary"),
                                             vmem_limit_bytes=VMEM_LIMIT),
        name="moba_attn",
    )(q, k, vt, ksum)


def _conv_kernel(z_ref, halo_ref, w_ref, bdw_ref, lng_ref, lnb_ref, cz_ref, zbuf, *, tiles_per_seq):
    tm = z_ref.shape[0]
    first = (pl.program_id(0) % tiles_per_seq) == 0
    zbuf[0:HALO, :] = jnp.where(first, 0.0, halo_ref[...])
    zbuf[HALO:HALO + tm, :] = z_ref[...]
    lead = HALO - (CONV_KERNEL - 1)
    for r in range(tm // CONV_ROWS):
        base = r * CONV_ROWS + lead
        acc = jnp.zeros((CONV_ROWS, CONV_WIDTH), F32) + bdw_ref[...]
        for t in range(CONV_KERNEL):
            acc = acc + w_ref[t:t + 1, :] * zbuf[base + t:base + t + CONV_ROWS, :]
        mu = jnp.mean(acc, axis=-1, keepdims=True)
        d = acc - mu
        var = jnp.mean(d * d, axis=-1, keepdims=True)
        y = (d * lax.rsqrt(var + EPS)) * lng_ref[...] + lnb_ref[...]
        cz_ref[r * CONV_ROWS:(r + 1) * CONV_ROWS, :] = (y * jax.nn.sigmoid(y)).astype(BF16)


def _conv_branch(z, w_dw, b_dw, ln_g, ln_b, seq):
    t = z.shape[0]
    tm = TM_CONV
    halo_blocks = tm // HALO
    return pl.pallas_call(
        functools.partial(_conv_kernel, tiles_per_seq=seq // tm),
        grid=(t // tm,),
        in_specs=[
            pl.BlockSpec((tm, CONV_WIDTH), lambda i: (i, 0)),
            pl.BlockSpec((HALO, CONV_WIDTH), lambda i: (jnp.maximum(i * halo_blocks - 1, 0), 0)),
            _const_spec((CONV_KERNEL, CONV_WIDTH)), _const_spec((1, CONV_WIDTH)),
            _const_spec((1, CONV_WIDTH)), _const_spec((1, CONV_WIDTH)),
        ],
        out_specs=pl.BlockSpec((tm, CONV_WIDTH), lambda i: (i, 0)),
        out_shape=jax.ShapeDtypeStruct((t, CONV_WIDTH), BF16),
        scratch_shapes=[pltpu.VMEM((HALO + tm, CONV_WIDTH), F32)],
        compiler_params=pltpu.CompilerParams(dimension_semantics=("arbitrary",),
                                             vmem_limit_bytes=VMEM_LIMIT),
        name="conv_branch",
    )(z, z, w_dw, b_dw, ln_g, ln_b)


def _mix_ffn_kernel(o_ref, cz_ref, ga_ref, gc_ref, x_ref, wup_ref, wpw_ref, bpw_ref, wout_ref,
                    gffn_ref, w1_ref, w2_ref, gfin_ref, out_ref, t_ref, *, final_norm):
    def rms(v, g_ref):
        ms = jnp.mean(v * v, axis=-1, keepdims=True)
        return (v * lax.rsqrt(ms + EPS)) * g_ref[...]

    branch_a = jnp.dot(o_ref[...], wup_ref[...], preferred_element_type=F32)
    branch_c = jnp.dot(cz_ref[...], wpw_ref[...], preferred_element_type=F32) + bpw_ref[...]
    merged = ga_ref[...].astype(F32) * branch_a + gc_ref[...].astype(F32) * branch_c
    x1 = x_ref[...] + jnp.dot(merged.astype(BF16), wout_ref[...], preferred_element_type=F32)
    h = rms(x1, gffn_ref).astype(BF16)
    chunk = D_MODEL
    for c in range(D_FF // chunk):
        u = jnp.dot(h, w1_ref[:, c * chunk:(c + 1) * chunk], preferred_element_type=F32)
        t_ref[:, c * chunk:(c + 1) * chunk] = jnp.square(jnp.maximum(u, 0.0)).astype(BF16)
    x2 = x1 + jnp.dot(t_ref[...], w2_ref[...], preferred_element_type=F32)
    out_ref[...] = rms(x2, gfin_ref) if final_norm else x2


def _mix_ffn(o, cz, ga, gc, x, wup, wpw, bpw, wout, gffn, w1, w2, gfin, final_norm):
    t = x.shape[0]
    tm = TM_FFN
    row = lambda w_: pl.BlockSpec((tm, w_), lambda i: (i, 0))
    return pl.pallas_call(
        functools.partial(_mix_ffn_kernel, final_norm=final_norm),
        grid=(t // tm,),
        in_specs=[row(ATTN_WIDTH), row(CONV_WIDTH), row(D_MODEL), row(D_MODEL), row(D_MODEL),
                  _const_spec((ATTN_WIDTH, D_MODEL)), _const_spec((CONV_WIDTH, D_MODEL)),
                  _const_spec((1, D_MODEL)), _const_spec((D_MODEL, D_MODEL)),
                  _const_spec((1, D_MODEL)), _const_spec((D_MODEL, D_FF)),
                  _const_spec((D_FF, D_MODEL)), _const_spec((1, D_MODEL))],
        out_specs=row(D_MODEL),
        out_shape=jax.ShapeDtypeStruct((t, D_MODEL), F32),
        scratch_shapes=[pltpu.VMEM((tm, D_FF), BF16)],
        compiler_params=pltpu.CompilerParams(dimension_semantics=("arbitrary",),
                                             vmem_limit_bytes=VMEM_LIMIT),
        name="mix_ffn",
    )(o, cz, ga, gc, x, wup, wpw, bpw, wout, gffn, w1, w2, gfin)


def kernel(x, g_mix, w_in, b_glu, w_dw, b_dw, ln_g, ln_b, w_conv_pw, b_conv_pw,
           w_attn_up, w_out, g_ffn, w_ff1, w_ff2, g_final):
    b, s, d = x.shape
    depth = w_in.shape[0]
    nb = s // MOBA_BLOCK
    assert d == D_MODEL and s % TM_INPROJ == 0 and TM_INPROJ % MOBA_BLOCK == 0
    xf = x.reshape(b * s, d)
    row = lambda a: a.reshape(1, -1)
    for l in range(depth):
        q, k, vt, z, ga, gc, ksum = _inproj(xf, row(g_mix[l]), w_in[l].astype(BF16), row(b_glu[l]), b, s)
        o = _attention(q.reshape(b, s, ATTN_WIDTH), k, vt, ksum.reshape(b, nb, ATTN_WIDTH))
        cz = _conv_branch(z, w_dw[l], row(b_dw[l]), row(ln_g[l]), row(ln_b[l]), s)
        xf = _mix_ffn(o.reshape(b * s, ATTN_WIDTH), cz, ga, gc, xf,
                      w_attn_up[l].astype(BF16), w_conv_pw[l].astype(BF16), row(b_conv_pw[l]),
                      w_out[l].astype(BF16), row(g_ffn[l]), w_ff1[l].astype(BF16),
                      w_ff2[l].astype(BF16), row(g_final), final_norm=(l == depth - 1))
    return xf.reshape(b, s, d)
```

```python
import functools
import math

import jax
import jax.numpy as jnp
from jax import lax
from jax.experimental import pallas as pl
from jax.experimental.pallas import tpu as pltpu

D_MODEL = 1024
N_HEADS = 8
HEAD_DIM = 64
ATTN_WIDTH = N_HEADS * HEAD_DIM
CONV_WIDTH = D_MODEL // 2
CONV_KERNEL = 31
MOBA_BLOCK = 256
MOBA_TOPK = 3
D_FF = 4 * D_MODEL
EPS = 1e-6
IN_COLS = 3 * ATTN_WIDTH + 2 * CONV_WIDTH + 2 * D_MODEL

LANES = 128
SUBLANES = 8
PAIR = LANES // HEAD_DIM
N_PAIRS = N_HEADS // PAIR
K_LANES = 2 * LANES
V_ROWS = HEAD_DIM + 16
Q_GROUP = 4
MASK_BIAS = -(2.0 ** 100)
M_INIT = -(2.0 ** 99)
Q_SCALE = HEAD_DIM ** -0.5 * math.log2(math.e)
GATE_CHUNK = 8
HALO = 32
CONV_ROWS = 64
TM_INPROJ = 512
TM_CONV = 256
TM_FFN = 256
VMEM_LIMIT = 56 * 1024 * 1024

F32 = jnp.float32
BF16 = jnp.bfloat16
NEG_INF = float("-inf")


def _const_spec(shape):
    return pl.BlockSpec(shape, lambda *_: (0,) * len(shape), pipeline_mode=pl.Buffered(1))


def _inproj_kernel(x_ref, g_ref, w_ref, bglu_ref,
                   q_ref, k_ref, vt_ref, z_ref, ga_ref, gc_ref, ksum_ref):
    x = x_ref[...]
    tm = x.shape[0]
    blocks = tm // MOBA_BLOCK
    ms = jnp.mean(x * x, axis=-1, keepdims=True)
    h = ((x * lax.rsqrt(ms + EPS)) * g_ref[...]).astype(BF16)

    def proj(lo, hi):
        return jnp.dot(h, w_ref[:, lo:hi], preferred_element_type=F32)

    c_q, c_k, c_v = ATTN_WIDTH, 2 * ATTN_WIDTH, 3 * ATTN_WIDTH
    c_a = c_v + CONV_WIDTH
    c_u = c_v + 2 * CONV_WIDTH
    c_ga = c_u + D_MODEL

    q_ref[...] = (proj(0, c_q) * Q_SCALE).astype(BF16)

    k = proj(c_q, c_k)
    sel_lane = lax.broadcasted_iota(jnp.int32, (MOBA_BLOCK, LANES), 1)
    for j in range(blocks):
        kb = k[j * MOBA_BLOCK:(j + 1) * MOBA_BLOCK]
        ksum_ref[0, j] = jnp.sum(kb, axis=0, keepdims=True)
        onehot = (sel_lane == pl.program_id(1) * blocks + j).astype(BF16)
        for p in range(N_PAIRS):
            k_ref[0, p, j, :, 0:LANES] = kb[:, p * LANES:(p + 1) * LANES].astype(BF16)
            k_ref[0, p, j, :, LANES:K_LANES] = onehot

    vt = proj(c_k, c_v).T
    ones_row = (lax.broadcasted_iota(jnp.int32, (V_ROWS - HEAD_DIM, MOBA_BLOCK), 0) == 0).astype(BF16)
    for hd in range(N_HEADS):
        for j in range(blocks):
            vt_ref[0, hd, j, 0:HEAD_DIM, :] = vt[hd * HEAD_DIM:(hd + 1) * HEAD_DIM,
                                                 j * MOBA_BLOCK:(j + 1) * MOBA_BLOCK].astype(BF16)
            vt_ref[0, hd, j, HEAD_DIM:V_ROWS, :] = ones_row

    a = proj(c_v, c_a) + bglu_ref[:, :CONV_WIDTH]
    gg = proj(c_a, c_u) + bglu_ref[:, CONV_WIDTH:]
    z_ref[...] = a * jax.nn.sigmoid(gg)
    ga_ref[...] = jax.nn.sigmoid(proj(c_u, c_ga)).astype(BF16)
    gc_ref[...] = jax.nn.sigmoid(proj(c_ga, IN_COLS)).astype(BF16)


def _inproj(x, g, w, bglu, batch, seq):
    t = x.shape[0]
    tm = TM_INPROJ
    nt = seq // tm
    nb = seq // MOBA_BLOCK
    bpt = tm // MOBA_BLOCK
    row = lambda w_: pl.BlockSpec((tm, w_), lambda b, i: (b * nt + i, 0))
    return pl.pallas_call(
        _inproj_kernel,
        grid=(batch, nt),
        in_specs=[row(D_MODEL), _const_spec((1, D_MODEL)), _const_spec((D_MODEL, IN_COLS)),
                  _const_spec((1, 2 * CONV_WIDTH))],
        out_specs=[
            row(ATTN_WIDTH),
            pl.BlockSpec((1, N_PAIRS, bpt, MOBA_BLOCK, K_LANES), lambda b, i: (b, 0, i, 0, 0)),
            pl.BlockSpec((1, N_HEADS, bpt, V_ROWS, MOBA_BLOCK), lambda b, i: (b, 0, i, 0, 0)),
            row(CONV_WIDTH), row(D_MODEL), row(D_MODEL),
            pl.BlockSpec((1, bpt, 1, ATTN_WIDTH), lambda b, i: (b, i, 0, 0)),
        ],
        out_shape=[
            jax.ShapeDtypeStruct((t, ATTN_WIDTH), BF16),
            jax.ShapeDtypeStruct((batch, N_PAIRS, nb, MOBA_BLOCK, K_LANES), BF16),
            jax.ShapeDtypeStruct((batch, N_HEADS, nb, V_ROWS, MOBA_BLOCK), BF16),
            jax.ShapeDtypeStruct((t, CONV_WIDTH), F32),
            jax.ShapeDtypeStruct((t, D_MODEL), BF16),
            jax.ShapeDtypeStruct((t, D_MODEL), BF16),
            jax.ShapeDtypeStruct((batch, nb, 1, ATTN_WIDTH), F32),
        ],
        compiler_params=pltpu.CompilerParams(dimension_semantics=("arbitrary", "arbitrary"),
                                             vmem_limit_bytes=VMEM_LIMIT),
        name="inproj",
    )(x, g, w, bglu)


def _attn_kernel(q_ref, k_ref, vt_ref, ksum_ref, o_ref, qa_ref, m_ref, acc_ref, s_ref, p_ref, *, n_blocks):
    blk_len = MOBA_BLOCK
    n_pad = qa_ref.shape[0]
    sel_rows = slice(LANES, LANES + n_blocks)

    kmean = ksum_ref[0] * (1.0 / blk_len)
    km_hi = kmean.astype(BF16)
    r1 = kmean - km_hi.astype(F32)
    km_mid = r1.astype(BF16)
    km_lo = (r1 - km_mid.astype(F32)).astype(BF16)
    km3 = jnp.concatenate([km_hi, km_mid, km_lo], axis=0)
    width = GATE_CHUNK * blk_len
    row = lax.broadcasted_iota(jnp.int32, (LANES, width), 0)
    blk = lax.broadcasted_iota(jnp.int32, (n_blocks, width), 0)
    col_blk = lax.broadcasted_iota(jnp.int32, (n_blocks, width), 1) // blk_len

    def gate_chunk(c, carry):
        qt = q_ref[0, pl.ds(pl.multiple_of(c * width, width), width), :].astype(F32).T
        q_blk = col_blk + c * GATE_CHUNK
        for hd in range(PAIR):
            in_head = (row >= HEAD_DIM * hd) & (row < HEAD_DIM * (hd + 1))
            qht = jnp.where(in_head, qt, 0.0).astype(BF16)
            g3 = jnp.dot(km3, qht, preferred_element_type=F32)
            gate = g3[0:n_blocks] + g3[n_blocks:2 * n_blocks] + g3[2 * n_blocks:3 * n_blocks]
            g = jnp.where(blk < q_blk, gate, NEG_INF)
            keep = blk == q_blk
            for _ in range(MOBA_TOPK):
                mx = jnp.max(g, axis=0, keepdims=True)
                idx = jnp.min(jnp.where(g == mx, blk, n_blocks), axis=0, keepdims=True)
                pick = blk == idx
                keep = keep | (pick & jnp.isfinite(mx))
                g = jnp.where(pick, NEG_INF, g)
            bias_t = jnp.where(keep, 0.0, MASK_BIAS).astype(BF16)
            for t in range(GATE_CHUNK):
                cols = slice(t * blk_len, (t + 1) * blk_len)
                qa_ref[c * GATE_CHUNK + t, hd, 0:LANES, :] = qht[:, cols]
                qa_ref[c * GATE_CHUNK + t, hd, sel_rows, :] = bias_t[:, cols]
        return carry

    lax.fori_loop(0, n_blocks // GATE_CHUNK, gate_chunk, 0)

    for t in range(n_blocks, n_pad):
        for hd in range(PAIR):
            qa_ref[t, hd, 0:LANES, :] = jnp.zeros((LANES, blk_len), BF16)
            qa_ref[t, hd, sel_rows, :] = jnp.full((n_blocks, blk_len), MASK_BIAS, BF16)
    qa_ref[:, :, LANES + n_blocks:K_LANES, :] = jnp.zeros(
        (n_pad, PAIR, K_LANES - LANES - n_blocks, blk_len), BF16)
    m_ref[...] = jnp.full(m_ref.shape, M_INIT, F32)
    acc_ref[...] = jnp.zeros(acc_ref.shape, F32)

    n_tiles = Q_GROUP * PAIR
    n_steps = sum(-(-(n_blocks - j) // Q_GROUP) for j in range(n_blocks))
    assert n_steps % 2 == 0
    causal_gap = (lax.broadcasted_iota(jnp.int32, (blk_len, blk_len), 0)
                  - lax.broadcasted_iota(jnp.int32, (blk_len, blk_len), 1))

    def score_group(slot, j, qb0):
        for t in range(n_tiles):
            s_ref[slot, t] = jnp.dot(k_ref[0, 0, j], qa_ref[qb0 + t // PAIR, t % PAIR],
                                     preferred_element_type=F32)

    def softmax_group(slot, gi, qb0):
        max_gap = jnp.where(gi == 0, 0, blk_len)
        alphas = []
        for t in range(n_tiles):
            qoff, hd = divmod(t, PAIR)
            s = s_ref[slot, t]
            if qoff == 0:
                s = jnp.where(causal_gap <= max_gap, s, NEG_INF)
            m_old = m_ref[qb0 + qoff, hd]
            m_new = jnp.maximum(m_old, jnp.max(s, axis=0, keepdims=True))
            m_ref[qb0 + qoff, hd] = m_new
            alphas.append(jnp.exp2(m_old - m_new))
            p_ref[slot, t] = jnp.exp2(s - m_new).astype(BF16)
        return tuple(alphas)

    def pv_group(slot, j, qb0, alphas):
        for t in range(n_tiles):
            qoff, hd = divmod(t, PAIR)
            acc_ref[qb0 + qoff, hd] = alphas[t] * acc_ref[qb0 + qoff, hd] + jnp.dot(
                vt_ref[0, hd, j], p_ref[slot, t], preferred_element_type=F32)

    def step(slot, state):
        j, gi, j_prev, qb0_prev, alphas_prev = state
        last = (gi + 1) * Q_GROUP >= n_blocks - j
        j_next = jnp.minimum(jnp.where(last, j + 1, j), n_blocks - 1)
        gi_next = jnp.where(last, 0, gi + 1)
        qb0 = j + gi * Q_GROUP
        score_group(1 - slot, j_next, j_next + gi_next * Q_GROUP)
        alphas = softmax_group(slot, gi, qb0)
        pv_group(1 - slot, j_prev, qb0_prev, alphas_prev)
        return j_next, gi_next, j, qb0, alphas

    zero = jnp.int32(0)
    score_group(0, zero, zero)
    p_ref[1] = jnp.zeros(p_ref.shape[1:], BF16)
    start = (zero, zero, zero, zero, tuple(jnp.ones((1, blk_len), F32) for _ in range(n_tiles)))
    state = lax.fori_loop(0, n_steps // 2, lambda i, st: step(1, step(0, st)), start)
    _, _, j_last, qb0_last, alphas_last = state
    pv_group(1, j_last, qb0_last, alphas_last)

    def finalize(t, carry):
        pair = []
        for hd in range(PAIR):
            acc = acc_ref[t, hd]
            pair.append(acc[0:HEAD_DIM] / acc[HEAD_DIM:HEAD_DIM + 1])
        rows = pl.ds(pl.multiple_of(t * blk_len, blk_len), blk_len)
        o_ref[0, rows, :] = jnp.concatenate(pair, axis=0).T.astype(BF16)
        return carry

    lax.fori_loop(0, n_blocks, finalize, 0)


def _attention(q, k, vt, ksum):
    b, s, _ = q.shape
    nb = s // MOBA_BLOCK
    blk = MOBA_BLOCK
    n_pad = nb + Q_GROUP - 1
    assert LANES + nb <= K_LANES and nb % GATE_CHUNK == 0 and nb % 16 == 0
    return pl.pallas_call(
        functools.partial(_attn_kernel, n_blocks=nb),
        grid=(b, N_PAIRS),
        in_specs=[
            pl.BlockSpec((1, s, LANES), lambda bi, p: (bi, 0, p)),
            pl.BlockSpec((1, 1, nb, blk, K_LANES), lambda bi, p: (bi, p, 0, 0, 0)),
            pl.BlockSpec((1, PAIR, nb, V_ROWS, blk), lambda bi, p: (bi, p, 0, 0, 0)),
            pl.BlockSpec((1, nb, LANES), lambda bi, p: (bi, 0, p)),
        ],
        out_specs=pl.BlockSpec((1, s, LANES), lambda bi, p: (bi, 0, p)),
        out_shape=jax.ShapeDtypeStruct((b, s, ATTN_WIDTH), BF16),
        scratch_shapes=[pltpu.VMEM((n_pad, PAIR, K_LANES, blk), BF16),
                        pltpu.VMEM((n_pad, PAIR, 1, blk), F32),
                        pltpu.VMEM((n_pad, PAIR, V_ROWS, blk), F32),
                        pltpu.VMEM((2, Q_GROUP * PAIR, blk, blk), F32),
                        pltpu.VMEM((2, Q_GROUP * PAIR, blk, blk), BF16)],
        compiler_params=pltpu.CompilerParams(dimension_semantics=("arbitrary", "arbitrary"),
                                             vmem_limit_bytes=VMEM_LIMIT),
        name="moba_attn",
    )(q, k, vt, ksum)


def _conv_kernel(z_ref, halo_ref, w_ref, bdw_ref, lng_ref, lnb_ref, cz_ref, zbuf, *, tiles_per_seq):
    tm = z_ref.shape[0]
    first = (pl.program_id(0) % tiles_per_seq) == 0
    zbuf[0:HALO, :] = jnp.where(first, 0.0, halo_ref[...])
    zbuf[HALO:HALO + tm, :] = z_ref[...]
    zbuf[HALO + tm:HALO + tm + SUBLANES, :] = jnp.zeros((SUBLANES, CONV_WIDTH), F32)
    lead = HALO - (CONV_KERNEL - 1)
    for r in range(tm // CONV_ROWS):
        base = r * CONV_ROWS
        acc = jnp.zeros((CONV_ROWS, CONV_WIDTH), F32) + bdw_ref[...]
        for phase in range(SUBLANES):
            offsets = [o for o in range(lead, lead + CONV_KERNEL) if o % SUBLANES == phase]
            part = None
            for o in offsets:
                start = base + o - phase
                term = w_ref[o - lead:o - lead + 1, :] * zbuf[start:start + CONV_ROWS + SUBLANES, :]
                part = term if part is None else part + term
            acc = acc + part[phase:phase + CONV_ROWS]
        mu = jnp.mean(acc, axis=-1, keepdims=True)
        d = acc - mu
        var = jnp.mean(d * d, axis=-1, keepdims=True)
        y = (d * lax.rsqrt(var + EPS)) * lng_ref[...] + lnb_ref[...]
        cz_ref[base:base + CONV_ROWS, :] = (y * jax.nn.sigmoid(y)).astype(BF16)


def _conv_branch(z, w_dw, b_dw, ln_g, ln_b, seq):
    t = z.shape[0]
    tm = TM_CONV
    halo_blocks = tm // HALO
    return pl.pallas_call(
        functools.partial(_conv_kernel, tiles_per_seq=seq // tm),
        grid=(t // tm,),
        in_specs=[
            pl.BlockSpec((tm, CONV_WIDTH), lambda i: (i, 0)),
            pl.BlockSpec((HALO, CONV_WIDTH), lambda i: (jnp.maximum(i * halo_blocks - 1, 0), 0)),
            _const_spec((CONV_KERNEL, CONV_WIDTH)), _const_spec((1, CONV_WIDTH)),
            _const_spec((1, CONV_WIDTH)), _const_spec((1, CONV_WIDTH)),
        ],
        out_specs=pl.BlockSpec((tm, CONV_WIDTH), lambda i: (i, 0)),
        out_shape=jax.ShapeDtypeStruct((t, CONV_WIDTH), BF16),
        scratch_shapes=[pltpu.VMEM((HALO + tm + SUBLANES, CONV_WIDTH), F32)],
        compiler_params=pltpu.CompilerParams(dimension_semantics=("arbitrary",),
                                             vmem_limit_bytes=VMEM_LIMIT),
        name="conv_branch",
    )(z, z, w_dw, b_dw, ln_g, ln_b)


def _mix_ffn_kernel(o_ref, cz_ref, ga_ref, gc_ref, x_ref, wup_ref, wpw_ref, bpw_ref, wout_ref,
                    gffn_ref, w1_ref, w2_ref, gfin_ref, out_ref, t_ref, *, final_norm):
    def rms(v, g_ref):
        ms = jnp.mean(v * v, axis=-1, keepdims=True)
        return (v * lax.rsqrt(ms + EPS)) * g_ref[...]

    branch_a = jnp.dot(o_ref[...], wup_ref[...], preferred_element_type=F32)
    branch_c = jnp.dot(cz_ref[...], wpw_ref[...], preferred_element_type=F32) + bpw_ref[...]
    merged = ga_ref[...].astype(F32) * branch_a + gc_ref[...].astype(F32) * branch_c
    x1 = x_ref[...] + jnp.dot(merged.astype(BF16), wout_ref[...], preferred_element_type=F32)
    h = rms(x1, gffn_ref).astype(BF16)
    chunk = D_MODEL
    for c in range(D_FF // chunk):
        u = jnp.dot(h, w1_ref[:, c * chunk:(c + 1) * chunk], preferred_element_type=F32)
        t_ref[:, c * chunk:(c + 1) * chunk] = jnp.square(jnp.maximum(u, 0.0)).astype(BF16)
    x2 = x1 + jnp.dot(t_ref[...], w2_ref[...], preferred_element_type=F32)
    out_ref[...] = rms(x2, gfin_ref) if final_norm else x2


def _mix_ffn(o, cz, ga, gc, x, wup, wpw, bpw, wout, gffn, w1, w2, gfin, final_norm):
    t = x.shape[0]
    tm = TM_FFN
    row = lambda w_: pl.BlockSpec((tm, w_), lambda i: (i, 0))
    return pl.pallas_call(
        functools.partial(_mix_ffn_kernel, final_norm=final_norm),
        grid=(t // tm,),
        in_specs=[row(ATTN_WIDTH), row(CONV_WIDTH), row(D_MODEL), row(D_MODEL), row(D_MODEL),
                  _const_spec((ATTN_WIDTH, D_MODEL)), _const_spec((CONV_WIDTH, D_MODEL)),
                  _const_spec((1, D_MODEL)), _const_spec((D_MODEL, D_MODEL)),
                  _const_spec((1, D_MODEL)), _const_spec((D_MODEL, D_FF)),
                  _const_spec((D_FF, D_MODEL)), _const_spec((1, D_MODEL))],
        out_specs=row(D_MODEL),
        out_shape=jax.ShapeDtypeStruct((t, D_MODEL), F32),
        scratch_shapes=[pltpu.VMEM((tm, D_FF), BF16)],
        compiler_params=pltpu.CompilerParams(dimension_semantics=("arbitrary",),
                                             vmem_limit_bytes=VMEM_LIMIT),
        name="mix_ffn",
    )(o, cz, ga, gc, x, wup, wpw, bpw, wout, gffn, w1, w2, gfin)


def kernel(x, g_mix, w_in, b_glu, w_dw, b_dw, ln_g, ln_b, w_conv_pw, b_conv_pw,
           w_attn_up, w_out, g_ffn, w_ff1, w_ff2, g_final):
    b, s, d = x.shape
    depth = w_in.shape[0]
    nb = s // MOBA_BLOCK
    assert d == D_MODEL and s % TM_INPROJ == 0 and TM_INPROJ % MOBA_BLOCK == 0
    xf = x.reshape(b * s, d)
    row = lambda a: a.reshape(1, -1)
    for l in range(depth):
        q, k, vt, z, ga, gc, ksum = _inproj(xf, row(g_mix[l]), w_in[l].astype(BF16), row(b_glu[l]), b, s)
        o = _attention(q.reshape(b, s, ATTN_WIDTH), k, vt, ksum.reshape(b, nb, ATTN_WIDTH))
        cz = _conv_branch(z, w_dw[l], row(b_dw[l]), row(ln_g[l]), row(ln_b[l]), s)
        xf = _mix_ffn(o.reshape(b * s, ATTN_WIDTH), cz, ga, gc, xf,
                      w_attn_up[l].astype(BF16), w_conv_pw[l].astype(BF16), row(b_conv_pw[l]),
                      w_out[l].astype(BF16), row(g_ffn[l]), w_ff1[l].astype(BF16),
                      w_ff2[l].astype(BF16), row(g_final), final_norm=(l == depth - 1))
    return xf.reshape(b, s, d)
```

```python
import functools
import math

import jax
import jax.numpy as jnp
from jax import lax
from jax.experimental import pallas as pl
from jax.experimental.pallas import tpu as pltpu

D_MODEL = 1024
N_HEADS = 8
HEAD_DIM = 64
ATTN_WIDTH = N_HEADS * HEAD_DIM
CONV_WIDTH = D_MODEL // 2
CONV_KERNEL = 31
MOBA_BLOCK = 256
MOBA_TOPK = 3
D_FF = 4 * D_MODEL
EPS = 1e-6
IN_COLS = 3 * ATTN_WIDTH + 2 * CONV_WIDTH + 2 * D_MODEL

LANES = 128
SUBLANES = 8
PAIR = LANES // HEAD_DIM
N_PAIRS = N_HEADS // PAIR
K_LANES = 2 * LANES
V_ROWS = HEAD_DIM + 16
Q_GROUP = 4
MASK_BIAS = -(2.0 ** 100)
M_INIT = -(2.0 ** 99)
Q_SCALE = HEAD_DIM ** -0.5 * math.log2(math.e)
GATE_CHUNK = 8
HALO = 32
CONV_ROWS = 64
TM_INPROJ = 512
TM_CONV = 256
TM_FFN = 256
VMEM_LIMIT = 56 * 1024 * 1024

F32 = jnp.float32
BF16 = jnp.bfloat16
NEG_INF = float("-inf")


def _const_spec(shape):
    return pl.BlockSpec(shape, lambda *_: (0,) * len(shape), pipeline_mode=pl.Buffered(1))


def _inproj_kernel(x_ref, g_ref, w_ref, bglu_ref,
                   q_ref, k_ref, vt_ref, z_ref, ga_ref, gc_ref, ksum_ref):
    x = x_ref[...]
    tm = x.shape[0]
    blocks = tm // MOBA_BLOCK
    ms = jnp.mean(x * x, axis=-1, keepdims=True)
    h = ((x * lax.rsqrt(ms + EPS)) * g_ref[...]).astype(BF16)

    def proj(lo, hi):
        return jnp.dot(h, w_ref[:, lo:hi], preferred_element_type=F32)

    c_q, c_k, c_v = ATTN_WIDTH, 2 * ATTN_WIDTH, 3 * ATTN_WIDTH
    c_a = c_v + CONV_WIDTH
    c_u = c_v + 2 * CONV_WIDTH
    c_ga = c_u + D_MODEL

    q_ref[...] = (proj(0, c_q) * Q_SCALE).astype(BF16)

    k = proj(c_q, c_k)
    sel_lane = lax.broadcasted_iota(jnp.int32, (MOBA_BLOCK, LANES), 1)
    for j in range(blocks):
        kb = k[j * MOBA_BLOCK:(j + 1) * MOBA_BLOCK]
        ksum_ref[0, j] = jnp.sum(kb, axis=0, keepdims=True)
        onehot = (sel_lane == pl.program_id(1) * blocks + j).astype(BF16)
        for p in range(N_PAIRS):
            k_ref[0, p, j, :, 0:LANES] = kb[:, p * LANES:(p + 1) * LANES].astype(BF16)
            k_ref[0, p, j, :, LANES:K_LANES] = onehot

    vt = proj(c_k, c_v).T
    ones_row = (lax.broadcasted_iota(jnp.int32, (V_ROWS - HEAD_DIM, MOBA_BLOCK), 0) == 0).astype(BF16)
    for hd in range(N_HEADS):
        for j in range(blocks):
            vt_ref[0, hd, j, 0:HEAD_DIM, :] = vt[hd * HEAD_DIM:(hd + 1) * HEAD_DIM,
                                                 j * MOBA_BLOCK:(j + 1) * MOBA_BLOCK].astype(BF16)
            vt_ref[0, hd, j, HEAD_DIM:V_ROWS, :] = ones_row

    a = proj(c_v, c_a) + bglu_ref[:, :CONV_WIDTH]
    gg = proj(c_a, c_u) + bglu_ref[:, CONV_WIDTH:]
    z_ref[...] = a * jax.nn.sigmoid(gg)
    ga_ref[...] = jax.nn.sigmoid(proj(c_u, c_ga)).astype(BF16)
    gc_ref[...] = jax.nn.sigmoid(proj(c_ga, IN_COLS)).astype(BF16)


def _inproj(x, g, w, bglu, batch, seq):
    t = x.shape[0]
    tm = TM_INPROJ
    nt = seq // tm
    nb = seq // MOBA_BLOCK
    bpt = tm // MOBA_BLOCK
    row = lambda w_: pl.BlockSpec((tm, w_), lambda b, i: (b * nt + i, 0))
    return pl.pallas_call(
        _inproj_kernel,
        grid=(batch, nt),
        in_specs=[row(D_MODEL), _const_spec((1, D_MODEL)), _const_spec((D_MODEL, IN_COLS)),
                  _const_spec((1, 2 * CONV_WIDTH))],
        out_specs=[
            row(ATTN_WIDTH),
            pl.BlockSpec((1, N_PAIRS, bpt, MOBA_BLOCK, K_LANES), lambda b, i: (b, 0, i, 0, 0)),
            pl.BlockSpec((1, N_HEADS, bpt, V_ROWS, MOBA_BLOCK), lambda b, i: (b, 0, i, 0, 0)),
            row(CONV_WIDTH), row(D_MODEL), row(D_MODEL),
            pl.BlockSpec((1, bpt, 1, ATTN_WIDTH), lambda b, i: (b, i, 0, 0)),
        ],
        out_shape=[
            jax.ShapeDtypeStruct((t, ATTN_WIDTH), BF16),
            jax.ShapeDtypeStruct((batch, N_PAIRS, nb, MOBA_BLOCK, K_LANES), BF16),
            jax.ShapeDtypeStruct((batch, N_HEADS, nb, V_ROWS, MOBA_BLOCK), BF16),
            jax.ShapeDtypeStruct((t, CONV_WIDTH), F32),
            jax.ShapeDtypeStruct((t, D_MODEL), BF16),
            jax.ShapeDtypeStruct((t, D_MODEL), BF16),
            jax.ShapeDtypeStruct((batch, nb, 1, ATTN_WIDTH), F32),
        ],
        compiler_params=pltpu.CompilerParams(dimension_semantics=("arbitrary", "arbitrary"),
                                             vmem_limit_bytes=VMEM_LIMIT),
        name="inproj",
    )(x, g, w, bglu)


def _attn_kernel(q_ref, k_ref, vt_ref, ksum_ref, o_ref, qa_ref, m_ref, acc_ref, s_ref, p_ref, *, n_blocks):
    blk_len = MOBA_BLOCK
    n_pad = qa_ref.shape[0]
    sel_rows = slice(LANES, LANES + n_blocks)

    kmean = ksum_ref[0] * (1.0 / blk_len)
    km_hi = kmean.astype(BF16)
    r1 = kmean - km_hi.astype(F32)
    km_mid = r1.astype(BF16)
    km_lo = (r1 - km_mid.astype(F32)).astype(BF16)
    km3 = jnp.concatenate([km_hi, km_mid, km_lo], axis=0)
    width = GATE_CHUNK * blk_len
    row = lax.broadcasted_iota(jnp.int32, (LANES, width), 0)
    blk = lax.broadcasted_iota(jnp.int32, (n_blocks, width), 0)
    col_blk = lax.broadcasted_iota(jnp.int32, (n_blocks, width), 1) // blk_len

    def gate_chunk(c, carry):
        qt = q_ref[0, pl.ds(pl.multiple_of(c * width, width), width), :].astype(F32).T
        q_blk = col_blk + c * GATE_CHUNK
        for hd in range(PAIR):
            in_head = (row >= HEAD_DIM * hd) & (row < HEAD_DIM * (hd + 1))
            qht = jnp.where(in_head, qt, 0.0).astype(BF16)
            g3 = jnp.dot(km3, qht, preferred_element_type=F32)
            gate = g3[0:n_blocks] + g3[n_blocks:2 * n_blocks] + g3[2 * n_blocks:3 * n_blocks]
            g = jnp.where(blk < q_blk, gate, NEG_INF)
            keep = blk == q_blk
            for _ in range(MOBA_TOPK):
                mx = jnp.max(g, axis=0, keepdims=True)
                idx = jnp.min(jnp.where(g == mx, blk, n_blocks), axis=0, keepdims=True)
                pick = blk == idx
                keep = keep | (pick & jnp.isfinite(mx))
                g = jnp.where(pick, NEG_INF, g)
            bias_t = jnp.where(keep, 0.0, MASK_BIAS).astype(BF16)
            for t in range(GATE_CHUNK):
                cols = slice(t * blk_len, (t + 1) * blk_len)
                qa_ref[c * GATE_CHUNK + t, hd, 0:LANES, :] = qht[:, cols]
                qa_ref[c * GATE_CHUNK + t, hd, sel_rows, :] = bias_t[:, cols]
        return carry

    lax.fori_loop(0, n_blocks // GATE_CHUNK, gate_chunk, 0)

    for t in range(n_blocks, n_pad):
        for hd in range(PAIR):
            qa_ref[t, hd, 0:LANES, :] = jnp.zeros((LANES, blk_len), BF16)
            qa_ref[t, hd, sel_rows, :] = jnp.full((n_blocks, blk_len), MASK_BIAS, BF16)
    qa_ref[:, :, LANES + n_blocks:K_LANES, :] = jnp.zeros(
        (n_pad, PAIR, K_LANES - LANES - n_blocks, blk_len), BF16)
    m_ref[...] = jnp.full(m_ref.shape, M_INIT, F32)
    acc_ref[...] = jnp.zeros(acc_ref.shape, F32)

    n_tiles = Q_GROUP * PAIR
    n_steps = sum(-(-(n_blocks - j) // Q_GROUP) for j in range(n_blocks))
    assert n_steps % 2 == 0
    causal_gap = (lax.broadcasted_iota(jnp.int32, (blk_len, blk_len), 0)
                  - lax.broadcasted_iota(jnp.int32, (blk_len, blk_len), 1))

    def score_tile(slot, t, j, gi, qb0):
        qoff, hd = divmod(t, PAIR)
        s = jnp.dot(k_ref[0, 0, j], qa_ref[qb0 + qoff, hd], preferred_element_type=F32)
        if qoff == 0:
            s = jnp.where(causal_gap <= jnp.where(gi == 0, 0, blk_len), s, NEG_INF)
        s_ref[slot, t] = s
        return jnp.max(s, axis=0, keepdims=True)

    def softmax_tile(slot, t, qb0, col_max):
        qoff, hd = divmod(t, PAIR)
        m_old = m_ref[qb0 + qoff, hd]
        m_new = jnp.maximum(m_old, col_max)
        m_ref[qb0 + qoff, hd] = m_new
        p_ref[slot, t] = jnp.exp2(s_ref[slot, t] - m_new).astype(BF16)
        return jnp.exp2(m_old - m_new)

    def pv_tile(slot, t, j, qb0, alpha):
        qoff, hd = divmod(t, PAIR)
        acc_ref[qb0 + qoff, hd] = alpha * acc_ref[qb0 + qoff, hd] + jnp.dot(
            vt_ref[0, hd, j], p_ref[slot, t], preferred_element_type=F32)

    def step(slot, state):
        j, gi, col_max, j_prev, qb0_prev, alphas_prev = state
        last = (gi + 1) * Q_GROUP >= n_blocks - j
        j_next = jnp.minimum(jnp.where(last, j + 1, j), n_blocks - 1)
        gi_next = jnp.where(last, 0, gi + 1)
        qb0 = j + gi * Q_GROUP
        qb0_next = j_next + gi_next * Q_GROUP
        alphas = [softmax_tile(slot, t, qb0, col_max[t]) for t in range(n_tiles)]
        col_max_next = [score_tile(1 - slot, t, j_next, gi_next, qb0_next) for t in range(n_tiles)]
        for t in range(n_tiles):
            pv_tile(1 - slot, t, j_prev, qb0_prev, alphas_prev[t])
        return j_next, gi_next, tuple(col_max_next), j, qb0, tuple(alphas)

    zero = jnp.int32(0)
    col_max0 = tuple(score_tile(0, t, zero, zero, zero) for t in range(n_tiles))
    p_ref[1] = jnp.zeros(p_ref.shape[1:], BF16)
    start = (zero, zero, col_max0, zero, zero, tuple(jnp.ones((1, blk_len), F32) for _ in range(n_tiles)))
    state = lax.fori_loop(0, n_steps // 2, lambda i, st: step(1, step(0, st)), start)
    j_last, qb0_last, alphas_last = state[3:]
    for t in range(n_tiles):
        pv_tile(1, t, j_last, qb0_last, alphas_last[t])

    def finalize(c, carry):
        outs = []
        for u in range(Q_GROUP):
            for hd in range(PAIR):
                acc = acc_ref[c * Q_GROUP + u, hd]
                outs.append(acc[0:HEAD_DIM] / acc[HEAD_DIM:HEAD_DIM + 1])
        width = Q_GROUP * blk_len
        rows = pl.ds(pl.multiple_of(c * width, width), width)
        tiles = [jnp.concatenate(outs[u * PAIR:(u + 1) * PAIR], axis=0) for u in range(Q_GROUP)]
        o_ref[0, rows, :] = jnp.concatenate(tiles, axis=1).T.astype(BF16)
        return carry

    lax.fori_loop(0, n_blocks // Q_GROUP, finalize, 0)


def _attention(q, k, vt, ksum):
    b, s, _ = q.shape
    nb = s // MOBA_BLOCK
    blk = MOBA_BLOCK
    n_pad = nb + Q_GROUP - 1
    assert LANES + nb <= K_LANES and nb % GATE_CHUNK == 0 and nb % 16 == 0
    return pl.pallas_call(
        functools.partial(_attn_kernel, n_blocks=nb),
        grid=(b, N_PAIRS),
        in_specs=[
            pl.BlockSpec((1, s, LANES), lambda bi, p: (bi, 0, p)),
            pl.BlockSpec((1, 1, nb, blk, K_LANES), lambda bi, p: (bi, p, 0, 0, 0)),
            pl.BlockSpec((1, PAIR, nb, V_ROWS, blk), lambda bi, p: (bi, p, 0, 0, 0)),
            pl.BlockSpec((1, nb, LANES), lambda bi, p: (bi, 0, p)),
        ],
        out_specs=pl.BlockSpec((1, s, LANES), lambda bi, p: (bi, 0, p)),
        out_shape=jax.ShapeDtypeStruct((b, s, ATTN_WIDTH), BF16),
        scratch_shapes=[pltpu.VMEM((n_pad, PAIR, K_LANES, blk), BF16),
                        pltpu.VMEM((n_pad, PAIR, 1, blk), F32),
                        pltpu.VMEM((n_pad, PAIR, V_ROWS, blk), F32),
                        pltpu.VMEM((2, Q_GROUP * PAIR, blk, blk), F32),
                        pltpu.VMEM((2, Q_GROUP * PAIR, blk, blk), BF16)],
        compiler_params=pltpu.CompilerParams(dimension_semantics=("arbitrary", "arbitrary"),
                                             vmem_limit_bytes=VMEM_LIMIT),
        name="moba_attn",
    )(q, k, vt, ksum)


def _conv_kernel(z_ref, halo_ref, w_ref, bdw_ref, lng_ref, lnb_ref, cz_ref, zbuf, *, tiles_per_seq):
    tm = z_ref.shape[0]
    first = (pl.program_id(0) % tiles_per_seq) == 0
    zbuf[0:HALO, :] = jnp.where(first, 0.0, halo_ref[...])
    zbuf[HALO:HALO + tm, :] = z_ref[...]
    zbuf[HALO + tm:HALO + tm + SUBLANES, :] = jnp.zeros((SUBLANES, CONV_WIDTH), F32)
    lead = HALO - (CONV_KERNEL - 1)
    for r in range(tm // CONV_ROWS):
        base = r * CONV_ROWS
        acc = jnp.zeros((CONV_ROWS, CONV_WIDTH), F32) + bdw_ref[...]
        for phase in range(SUBLANES):
            offsets = [o for o in range(lead, lead + CONV_KERNEL) if o % SUBLANES == phase]
            part = None
            for o in offsets:
                start = base + o - phase
                term = w_ref[o - lead:o - lead + 1, :] * zbuf[start:start + CONV_ROWS + SUBLANES, :]
                part = term if part is None else part + term
            acc = acc + part[phase:phase + CONV_ROWS]
        mu = jnp.mean(acc, axis=-1, keepdims=True)
        d = acc - mu
        var = jnp.mean(d * d, axis=-1, keepdims=True)
        y = (d * lax.rsqrt(var + EPS)) * lng_ref[...] + lnb_ref[...]
        cz_ref[base:base + CONV_ROWS, :] = (y * jax.nn.sigmoid(y)).astype(BF16)


def _conv_branch(z, w_dw, b_dw, ln_g, ln_b, seq):
    t = z.shape[0]
    tm = TM_CONV
    halo_blocks = tm // HALO
    return pl.pallas_call(
        functools.partial(_conv_kernel, tiles_per_seq=seq // tm),
        grid=(t // tm,),
        in_specs=[
            pl.BlockSpec((tm, CONV_WIDTH), lambda i: (i, 0)),
            pl.BlockSpec((HALO, CONV_WIDTH), lambda i: (jnp.maximum(i * halo_blocks - 1, 0), 0)),
            _const_spec((CONV_KERNEL, CONV_WIDTH)), _const_spec((1, CONV_WIDTH)),
            _const_spec((1, CONV_WIDTH)), _const_spec((1, CONV_WIDTH)),
        ],
        out_specs=pl.BlockSpec((tm, CONV_WIDTH), lambda i: (i, 0)),
        out_shape=jax.ShapeDtypeStruct((t, CONV_WIDTH), BF16),
        scratch_shapes=[pltpu.VMEM((HALO + tm + SUBLANES, CONV_WIDTH), F32)],
        compiler_params=pltpu.CompilerParams(dimension_semantics=("arbitrary",),
                                             vmem_limit_bytes=VMEM_LIMIT),
        name="conv_branch",
    )(z, z, w_dw, b_dw, ln_g, ln_b)


def _mix_ffn_kernel(o_ref, cz_ref, ga_ref, gc_ref, x_ref, wup_ref, wpw_ref, bpw_ref, wout_ref,
                    gffn_ref, w1_ref, w2_ref, gfin_ref, out_ref, t_ref, *, final_norm):
    def rms(v, g_ref):
        ms = jnp.mean(v * v, axis=-1, keepdims=True)
        return (v * lax.rsqrt(ms + EPS)) * g_ref[...]

    branch_a = jnp.dot(o_ref[...], wup_ref[...], preferred_element_type=F32)
    branch_c = jnp.dot(cz_ref[...], wpw_ref[...], preferred_element_type=F32) + bpw_ref[...]
    merged = ga_ref[...].astype(F32) * branch_a + gc_ref[...].astype(F32) * branch_c
    x1 = x_ref[...] + jnp.dot(merged.astype(BF16), wout_ref[...], preferred_element_type=F32)
    h = rms(x1, gffn_ref).astype(BF16)
    chunk = D_MODEL
    for c in range(D_FF // chunk):
        u = jnp.dot(h, w1_ref[:, c * chunk:(c + 1) * chunk], preferred_element_type=F32)
        t_ref[:, c * chunk:(c + 1) * chunk] = jnp.square(jnp.maximum(u, 0.0)).astype(BF16)
    x2 = x1 + jnp.dot(t_ref[...], w2_ref[...], preferred_element_type=F32)
    out_ref[...] = rms(x2, gfin_ref) if final_norm else x2


def _mix_ffn(o, cz, ga, gc, x, wup, wpw, bpw, wout, gffn, w1, w2, gfin, final_norm):
    t = x.shape[0]
    tm = TM_FFN
    row = lambda w_: pl.BlockSpec((tm, w_), lambda i: (i, 0))
    return pl.pallas_call(
        functools.partial(_mix_ffn_kernel, final_norm=final_norm),
        grid=(t // tm,),
        in_specs=[row(ATTN_WIDTH), row(CONV_WIDTH), row(D_MODEL), row(D_MODEL), row(D_MODEL),
                  _const_spec((ATTN_WIDTH, D_MODEL)), _const_spec((CONV_WIDTH, D_MODEL)),
                  _const_spec((1, D_MODEL)), _const_spec((D_MODEL, D_MODEL)),
                  _const_spec((1, D_MODEL)), _const_spec((D_MODEL, D_FF)),
                  _const_spec((D_FF, D_MODEL)), _const_spec((1, D_MODEL))],
        out_specs=row(D_MODEL),
        out_shape=jax.ShapeDtypeStruct((t, D_MODEL), F32),
        scratch_shapes=[pltpu.VMEM((tm, D_FF), BF16)],
        compiler_params=pltpu.CompilerParams(dimension_semantics=("arbitrary",),
                                             vmem_limit_bytes=VMEM_LIMIT),
        name="mix_ffn",
    )(o, cz, ga, gc, x, wup, wpw, bpw, wout, gffn, w1, w2, gfin)


def kernel(x, g_mix, w_in, b_glu, w_dw, b_dw, ln_g, ln_b, w_conv_pw, b_conv_pw,
           w_attn_up, w_out, g_ffn, w_ff1, w_ff2, g_final):
    b, s, d = x.shape
    depth = w_in.shape[0]
    nb = s // MOBA_BLOCK
    assert d == D_MODEL and s % TM_INPROJ == 0 and TM_INPROJ % MOBA_BLOCK == 0
    xf = x.reshape(b * s, d)
    row = lambda a: a.reshape(1, -1)
    for l in range(depth):
        q, k, vt, z, ga, gc, ksum = _inproj(xf, row(g_mix[l]), w_in[l].astype(BF16), row(b_glu[l]), b, s)
        o = _attention(q.reshape(b, s, ATTN_WIDTH), k, vt, ksum.reshape(b, nb, ATTN_WIDTH))
        cz = _conv_branch(z, w_dw[l], row(b_dw[l]), row(ln_g[l]), row(ln_b[l]), s)
        xf = _mix_ffn(o.reshape(b * s, ATTN_WIDTH), cz, ga, gc, xf,
                      w_attn_up[l].astype(BF16), w_conv_pw[l].astype(BF16), row(b_conv_pw[l]),
                      w_out[l].astype(BF16), row(g_ffn[l]), w_ff1[l].astype(BF16),
                      w_ff2[l].astype(BF16), row(g_final), final_norm=(l == depth - 1))
    return xf.reshape(b, s, d)
```

```python
import functools
import math

import jax
import jax.numpy as jnp
from jax import lax
from jax.experimental import pallas as pl
from jax.experimental.pallas import tpu as pltpu

D_MODEL = 1024
N_HEADS = 8
HEAD_DIM = 64
ATTN_WIDTH = N_HEADS * HEAD_DIM
CONV_WIDTH = D_MODEL // 2
CONV_KERNEL = 31
MOBA_BLOCK = 256
MOBA_TOPK = 3
D_FF = 4 * D_MODEL
EPS = 1e-6
IN_COLS = 3 * ATTN_WIDTH + 2 * CONV_WIDTH + 2 * D_MODEL

LANES = 128
SUBLANES = 8
PAIR = LANES // HEAD_DIM
N_PAIRS = N_HEADS // PAIR
K_LANES = 2 * LANES
V_ROWS = HEAD_DIM + 16
Q_GROUP = 4
MASK_BIAS = -(2.0 ** 100)
M_INIT = -(2.0 ** 99)
Q_SCALE = HEAD_DIM ** -0.5 * math.log2(math.e)
GATE_CHUNK = 8
HALO = 32
CONV_ROWS = 64
TM_INPROJ = 512
TM_FFN = 256
VMEM_LIMIT = 56 * 1024 * 1024

F32 = jnp.float32
BF16 = jnp.bfloat16
NEG_INF = float("-inf")


def _const_spec(shape):
    return pl.BlockSpec(shape, lambda *_: (0,) * len(shape), pipeline_mode=pl.Buffered(1))


def _inproj_kernel(x_ref, g_ref, w_ref, bglu_ref,
                   q_ref, k_ref, vt_ref, z_ref, ga_ref, gc_ref, ksum_ref):
    x = x_ref[...]
    tm = x.shape[0]
    blocks = tm // MOBA_BLOCK
    ms = jnp.mean(x * x, axis=-1, keepdims=True)
    h = ((x * lax.rsqrt(ms + EPS)) * g_ref[...]).astype(BF16)

    def proj(lo, hi):
        return jnp.dot(h, w_ref[:, lo:hi], preferred_element_type=F32)

    c_q, c_k, c_v = ATTN_WIDTH, 2 * ATTN_WIDTH, 3 * ATTN_WIDTH
    c_a = c_v + CONV_WIDTH
    c_u = c_v + 2 * CONV_WIDTH
    c_ga = c_u + D_MODEL

    q_ref[...] = (proj(0, c_q) * Q_SCALE).astype(BF16)

    k = proj(c_q, c_k)
    sel_lane = lax.broadcasted_iota(jnp.int32, (MOBA_BLOCK, LANES), 1)
    for j in range(blocks):
        kb = k[j * MOBA_BLOCK:(j + 1) * MOBA_BLOCK]
        ksum_ref[0, j] = jnp.sum(kb, axis=0, keepdims=True)
        onehot = (sel_lane == pl.program_id(1) * blocks + j).astype(BF16)
        for p in range(N_PAIRS):
            k_ref[0, p, j, :, 0:LANES] = kb[:, p * LANES:(p + 1) * LANES].astype(BF16)
            k_ref[0, p, j, :, LANES:K_LANES] = onehot

    vt = proj(c_k, c_v).T
    ones_row = (lax.broadcasted_iota(jnp.int32, (V_ROWS - HEAD_DIM, MOBA_BLOCK), 0) == 0).astype(BF16)
    for hd in range(N_HEADS):
        for j in range(blocks):
            vt_ref[0, hd, j, 0:HEAD_DIM, :] = vt[hd * HEAD_DIM:(hd + 1) * HEAD_DIM,
                                                 j * MOBA_BLOCK:(j + 1) * MOBA_BLOCK].astype(BF16)
            vt_ref[0, hd, j, HEAD_DIM:V_ROWS, :] = ones_row

    a = proj(c_v, c_a) + bglu_ref[:, :CONV_WIDTH]
    gg = proj(c_a, c_u) + bglu_ref[:, CONV_WIDTH:]
    z_ref[...] = a * jax.nn.sigmoid(gg)
    ga_ref[...] = jax.nn.sigmoid(proj(c_u, c_ga)).astype(BF16)
    gc_ref[...] = jax.nn.sigmoid(proj(c_ga, IN_COLS)).astype(BF16)


def _inproj(x, g, w, bglu, batch, seq):
    t = x.shape[0]
    tm = TM_INPROJ
    nt = seq // tm
    nb = seq // MOBA_BLOCK
    bpt = tm // MOBA_BLOCK
    row = lambda w_: pl.BlockSpec((tm, w_), lambda b, i: (b * nt + i, 0))
    return pl.pallas_call(
        _inproj_kernel,
        grid=(batch, nt),
        in_specs=[row(D_MODEL), _const_spec((1, D_MODEL)), _const_spec((D_MODEL, IN_COLS)),
                  _const_spec((1, 2 * CONV_WIDTH))],
        out_specs=[
            row(ATTN_WIDTH),
            pl.BlockSpec((1, N_PAIRS, bpt, MOBA_BLOCK, K_LANES), lambda b, i: (b, 0, i, 0, 0)),
            pl.BlockSpec((1, N_HEADS, bpt, V_ROWS, MOBA_BLOCK), lambda b, i: (b, 0, i, 0, 0)),
            row(CONV_WIDTH), row(D_MODEL), row(D_MODEL),
            pl.BlockSpec((1, bpt, 1, ATTN_WIDTH), lambda b, i: (b, i, 0, 0)),
        ],
        out_shape=[
            jax.ShapeDtypeStruct((t, ATTN_WIDTH), BF16),
            jax.ShapeDtypeStruct((batch, N_PAIRS, nb, MOBA_BLOCK, K_LANES), BF16),
            jax.ShapeDtypeStruct((batch, N_HEADS, nb, V_ROWS, MOBA_BLOCK), BF16),
            jax.ShapeDtypeStruct((t, CONV_WIDTH), F32),
            jax.ShapeDtypeStruct((t, D_MODEL), BF16),
            jax.ShapeDtypeStruct((t, D_MODEL), BF16),
            jax.ShapeDtypeStruct((batch, nb, 1, ATTN_WIDTH), F32),
        ],
        compiler_params=pltpu.CompilerParams(dimension_semantics=("arbitrary", "arbitrary"),
                                             vmem_limit_bytes=VMEM_LIMIT),
        name="inproj",
    )(x, g, w, bglu)


def _tile_schedule(n_blocks, slots):
    n_pairs = n_blocks * (n_blocks + 1) // 2
    assert n_pairs % slots == 0
    remaining = {qb: list(range(qb)) for qb in range(1, n_blocks)}
    steps = []
    for s in range(n_pairs // slots):
        step = [(s, s)] if s < n_blocks else []
        used = {qb for _, qb in step}
        free = sorted((qb for qb, js in remaining.items() if js and qb not in used),
                      key=lambda qb: (-len(remaining[qb]), qb))
        step += [(remaining[qb].pop(0), qb) for qb in free[:slots - len(step)]]
        assert len(step) == slots
        steps.append(step)
    assert not any(remaining.values())
    return steps


def _attn_kernel(kt_ref, qt_ref, q_ref, k_ref, vt_ref, ksum_ref, o_ref, qa_ref, m_ref, acc_ref, s_ref, p_ref,
                 *, n_blocks, n_steps):
    blk_len = MOBA_BLOCK
    sel_rows = slice(LANES, LANES + n_blocks)

    kmean = ksum_ref[0] * (1.0 / blk_len)
    km_hi = kmean.astype(BF16)
    r1 = kmean - km_hi.astype(F32)
    km_mid = r1.astype(BF16)
    km_lo = (r1 - km_mid.astype(F32)).astype(BF16)
    km3 = jnp.concatenate([km_hi, km_mid, km_lo], axis=0)
    width = GATE_CHUNK * blk_len
    row = lax.broadcasted_iota(jnp.int32, (LANES, width), 0)
    blk = lax.broadcasted_iota(jnp.int32, (n_blocks, width), 0)
    col_blk = lax.broadcasted_iota(jnp.int32, (n_blocks, width), 1) // blk_len

    def gate_chunk(c, carry):
        qt = q_ref[0, pl.ds(pl.multiple_of(c * width, width), width), :].astype(F32).T
        q_blk = col_blk + c * GATE_CHUNK
        for hd in range(PAIR):
            in_head = (row >= HEAD_DIM * hd) & (row < HEAD_DIM * (hd + 1))
            qht = jnp.where(in_head, qt, 0.0).astype(BF16)
            g3 = jnp.dot(km3, qht, preferred_element_type=F32)
            gate = g3[0:n_blocks] + g3[n_blocks:2 * n_blocks] + g3[2 * n_blocks:3 * n_blocks]
            g = jnp.where(blk < q_blk, gate, NEG_INF)
            keep = blk == q_blk
            for _ in range(MOBA_TOPK):
                mx = jnp.max(g, axis=0, keepdims=True)
                idx = jnp.min(jnp.where(g == mx, blk, n_blocks), axis=0, keepdims=True)
                pick = blk == idx
                keep = keep | (pick & jnp.isfinite(mx))
                g = jnp.where(pick, NEG_INF, g)
            bias_t = jnp.where(keep, 0.0, MASK_BIAS).astype(BF16)
            for t in range(GATE_CHUNK):
                cols = slice(t * blk_len, (t + 1) * blk_len)
                qa_ref[c * GATE_CHUNK + t, hd, 0:LANES, :] = qht[:, cols]
                qa_ref[c * GATE_CHUNK + t, hd, sel_rows, :] = bias_t[:, cols]
        return carry

    lax.fori_loop(0, n_blocks // GATE_CHUNK, gate_chunk, 0)

    qa_ref[:, :, LANES + n_blocks:K_LANES, :] = jnp.zeros(
        (n_blocks, PAIR, K_LANES - LANES - n_blocks, blk_len), BF16)
    m_ref[...] = jnp.full(m_ref.shape, M_INIT, F32)
    acc_ref[...] = jnp.zeros(acc_ref.shape, F32)

    n_tiles = Q_GROUP * PAIR
    assert n_steps % 2 == 0
    causal_gap = (lax.broadcasted_iota(jnp.int32, (blk_len, blk_len), 0)
                  - lax.broadcasted_iota(jnp.int32, (blk_len, blk_len), 1))

    def tile_blocks(i, t):
        return kt_ref[i * Q_GROUP + t // PAIR], qt_ref[i * Q_GROUP + t // PAIR]

    def score_tile(slot, t, i):
        j, qb = tile_blocks(i, t)
        s = jnp.dot(k_ref[0, 0, j], qa_ref[qb, t % PAIR], preferred_element_type=F32)
        if t // PAIR == 0:
            s = jnp.where(causal_gap <= jnp.where(j == qb, 0, blk_len), s, NEG_INF)
        s_ref[slot, t] = s
        return jnp.max(s, axis=0, keepdims=True)

    def softmax_tile(slot, t, i, col_max):
        _, qb = tile_blocks(i, t)
        m_old = m_ref[qb, t % PAIR]
        m_new = jnp.maximum(m_old, col_max)
        m_ref[qb, t % PAIR] = m_new
        p_ref[slot, t] = jnp.exp2(s_ref[slot, t] - m_new).astype(BF16)
        return jnp.exp2(m_old - m_new)

    def pv_tile(slot, t, i, alpha):
        j, qb = tile_blocks(i, t)
        acc_ref[qb, t % PAIR] = alpha * acc_ref[qb, t % PAIR] + jnp.dot(
            vt_ref[0, t % PAIR, j], p_ref[slot, t], preferred_element_type=F32)

    def step(slot, i, state):
        col_max, alphas_prev = state
        alphas = [softmax_tile(slot, t, i, col_max[t]) for t in range(n_tiles)]
        col_max_next = [score_tile(1 - slot, t, i + 1) for t in range(n_tiles)]
        for t in range(n_tiles):
            pv_tile(1 - slot, t, jnp.maximum(i - 1, 0), alphas_prev[t])
        return tuple(col_max_next), tuple(alphas)

    col_max0 = tuple(score_tile(0, t, 0) for t in range(n_tiles))
    p_ref[1] = jnp.zeros(p_ref.shape[1:], BF16)
    start = (col_max0, tuple(jnp.ones((1, blk_len), F32) for _ in range(n_tiles)))
    _, alphas_last = lax.fori_loop(
        0, n_steps // 2, lambda it, st: step(1, 2 * it + 1, step(0, 2 * it, st)), start)
    for t in range(n_tiles):
        pv_tile(1, t, n_steps - 1, alphas_last[t])

    def finalize(c, carry):
        outs = []
        for u in range(Q_GROUP):
            for hd in range(PAIR):
                acc = acc_ref[c * Q_GROUP + u, hd]
                outs.append(acc[0:HEAD_DIM] / acc[HEAD_DIM:HEAD_DIM + 1])
        width = Q_GROUP * blk_len
        rows = pl.ds(pl.multiple_of(c * width, width), width)
        tiles = [jnp.concatenate(outs[u * PAIR:(u + 1) * PAIR], axis=0) for u in range(Q_GROUP)]
        o_ref[0, rows, :] = jnp.concatenate(tiles, axis=1).T.astype(BF16)
        return carry

    lax.fori_loop(0, n_blocks // Q_GROUP, finalize, 0)


def _attention(q, k, vt, ksum):
    b, s, _ = q.shape
    nb = s // MOBA_BLOCK
    blk = MOBA_BLOCK
    assert LANES + nb <= K_LANES and nb % GATE_CHUNK == 0 and nb % 16 == 0 and nb % Q_GROUP == 0
    steps = _tile_schedule(nb, Q_GROUP)
    steps.append(steps[-1])
    key_blocks = jnp.asarray([j for step in steps for j, _ in step], jnp.int32)
    query_blocks = jnp.asarray([qb for step in steps for _, qb in step], jnp.int32)
    smem = pl.BlockSpec(memory_space=pltpu.SMEM)
    return pl.pallas_call(
        functools.partial(_attn_kernel, n_blocks=nb, n_steps=len(steps) - 1),
        grid=(b, N_PAIRS),
        in_specs=[
            smem, smem,
            pl.BlockSpec((1, s, LANES), lambda bi, p: (bi, 0, p)),
            pl.BlockSpec((1, 1, nb, blk, K_LANES), lambda bi, p: (bi, p, 0, 0, 0)),
            pl.BlockSpec((1, PAIR, nb, V_ROWS, blk), lambda bi, p: (bi, p, 0, 0, 0)),
            pl.BlockSpec((1, nb, LANES), lambda bi, p: (bi, 0, p)),
        ],
        out_specs=pl.BlockSpec((1, s, LANES), lambda bi, p: (bi, 0, p)),
        out_shape=jax.ShapeDtypeStruct((b, s, ATTN_WIDTH), BF16),
        scratch_shapes=[pltpu.VMEM((nb, PAIR, K_LANES, blk), BF16),
                        pltpu.VMEM((nb, PAIR, 1, blk), F32),
                        pltpu.VMEM((nb, PAIR, V_ROWS, blk), F32),
                        pltpu.VMEM((2, Q_GROUP * PAIR, blk, blk), F32),
                        pltpu.VMEM((2, Q_GROUP * PAIR, blk, blk), BF16)],
        compiler_params=pltpu.CompilerParams(dimension_semantics=("arbitrary", "arbitrary"),
                                             vmem_limit_bytes=VMEM_LIMIT),
        name="moba_attn",
    )(key_blocks, query_blocks, q, k, vt, ksum)


def _conv_fill(z_ref, halo_ref, first, zbuf):
    tm = z_ref.shape[0]
    zbuf[0:HALO, :] = jnp.where(first, 0.0, halo_ref[...])
    zbuf[HALO:HALO + tm, :] = z_ref[...]
    zbuf[HALO + tm:HALO + tm + SUBLANES, :] = jnp.zeros((SUBLANES, CONV_WIDTH), F32)


def _conv_chunk(r, zbuf, w_ref, bdw_ref, lng_ref, lnb_ref):
    lead = HALO - (CONV_KERNEL - 1)
    base = r * CONV_ROWS
    acc = jnp.zeros((CONV_ROWS, CONV_WIDTH), F32) + bdw_ref[...]
    for phase in range(SUBLANES):
        offsets = [o for o in range(lead, lead + CONV_KERNEL) if o % SUBLANES == phase]
        part = None
        for o in offsets:
            start = base + o - phase
            term = w_ref[o - lead:o - lead + 1, :] * zbuf[start:start + CONV_ROWS + SUBLANES, :]
            part = term if part is None else part + term
        acc = acc + part[phase:phase + CONV_ROWS]
    mu = jnp.mean(acc, axis=-1, keepdims=True)
    d = acc - mu
    var = jnp.mean(d * d, axis=-1, keepdims=True)
    y = (d * lax.rsqrt(var + EPS)) * lng_ref[...] + lnb_ref[...]
    return (y * jax.nn.sigmoid(y)).astype(BF16)


def _mix_ffn_kernel(z_ref, halo_ref, wdw_ref, bdw_ref, lng_ref, lnb_ref,
                    o_ref, ga_ref, gc_ref, x_ref, wup_ref, wpw_ref, bpw_ref, wout_ref,
                    gffn_ref, w1_ref, w2_ref, gfin_ref, out_ref, t_ref, zbuf, cz_ref,
                    *, final_norm, n_tiles, tiles_per_seq):
    step = pl.program_id(0)
    tm = z_ref.shape[0]

    @pl.when(step == 0)
    def _():
        cz_ref[1] = jnp.zeros(cz_ref.shape[1:], BF16)

    conv_tile = jnp.minimum(step, n_tiles - 1)
    _conv_fill(z_ref, halo_ref, conv_tile % tiles_per_seq == 0, zbuf)
    for r in range(tm // CONV_ROWS):
        cz_ref[step % 2, r * CONV_ROWS:(r + 1) * CONV_ROWS, :] = _conv_chunk(
            r, zbuf, wdw_ref, bdw_ref, lng_ref, lnb_ref)

    def rms(v, g_ref):
        ms = jnp.mean(v * v, axis=-1, keepdims=True)
        return (v * lax.rsqrt(ms + EPS)) * g_ref[...]

    branch_a = jnp.dot(o_ref[...], wup_ref[...], preferred_element_type=F32)
    branch_c = jnp.dot(cz_ref[(step + 1) % 2], wpw_ref[...], preferred_element_type=F32) + bpw_ref[...]
    merged = ga_ref[...].astype(F32) * branch_a + gc_ref[...].astype(F32) * branch_c
    x1 = x_ref[...] + jnp.dot(merged.astype(BF16), wout_ref[...], preferred_element_type=F32)
    h = rms(x1, gffn_ref).astype(BF16)
    chunk = D_MODEL
    for c in range(D_FF // chunk):
        u = jnp.dot(h, w1_ref[:, c * chunk:(c + 1) * chunk], preferred_element_type=F32)
        t_ref[:, c * chunk:(c + 1) * chunk] = jnp.square(jnp.maximum(u, 0.0)).astype(BF16)
    x2 = x1 + jnp.dot(t_ref[...], w2_ref[...], preferred_element_type=F32)
    out_ref[...] = rms(x2, gfin_ref) if final_norm else x2


def _mix_ffn(z, wdw, bdw, lng, lnb, o, ga, gc, x, wup, wpw, bpw, wout, gffn, w1, w2, gfin, final_norm, seq):
    t = x.shape[0]
    tm = TM_FFN
    nt = t // tm
    halo_blocks = tm // HALO
    conv_row = lambda s: jnp.minimum(s, nt - 1)
    row = lambda w_: pl.BlockSpec((tm, w_), lambda s: (jnp.maximum(s - 1, 0), 0))
    return pl.pallas_call(
        functools.partial(_mix_ffn_kernel, final_norm=final_norm, n_tiles=nt, tiles_per_seq=seq // tm),
        grid=(nt + 1,),
        in_specs=[pl.BlockSpec((tm, CONV_WIDTH), lambda s: (conv_row(s), 0)),
                  pl.BlockSpec((HALO, CONV_WIDTH),
                               lambda s: (jnp.maximum(conv_row(s) * halo_blocks - 1, 0), 0)),
                  _const_spec((CONV_KERNEL, CONV_WIDTH)), _const_spec((1, CONV_WIDTH)),
                  _const_spec((1, CONV_WIDTH)), _const_spec((1, CONV_WIDTH)),
                  row(ATTN_WIDTH), row(D_MODEL), row(D_MODEL), row(D_MODEL),
                  _const_spec((ATTN_WIDTH, D_MODEL)), _const_spec((CONV_WIDTH, D_MODEL)),
                  _const_spec((1, D_MODEL)), _const_spec((D_MODEL, D_MODEL)),
                  _const_spec((1, D_MODEL)), _const_spec((D_MODEL, D_FF)),
                  _const_spec((D_FF, D_MODEL)), _const_spec((1, D_MODEL))],
        out_specs=row(D_MODEL),
        out_shape=jax.ShapeDtypeStruct((t, D_MODEL), F32),
        scratch_shapes=[pltpu.VMEM((tm, D_FF), BF16),
                        pltpu.VMEM((HALO + tm + SUBLANES, CONV_WIDTH), F32),
                        pltpu.VMEM((2, tm, CONV_WIDTH), BF16)],
        compiler_params=pltpu.CompilerParams(dimension_semantics=("arbitrary",),
                                             vmem_limit_bytes=VMEM_LIMIT),
        name="mix_ffn",
    )(z, z, wdw, bdw, lng, lnb, o, ga, gc, x, wup, wpw, bpw, wout, gffn, w1, w2, gfin)


def kernel(x, g_mix, w_in, b_glu, w_dw, b_dw, ln_g, ln_b, w_conv_pw, b_conv_pw,
           w_attn_up, w_out, g_ffn, w_ff1, w_ff2, g_final):
    b, s, d = x.shape
    depth = w_in.shape[0]
    nb = s // MOBA_BLOCK
    assert d == D_MODEL and s % TM_INPROJ == 0 and TM_INPROJ % MOBA_BLOCK == 0
    xf = x.reshape(b * s, d)
    row = lambda a: a.reshape(1, -1)
    for l in range(depth):
        q, k, vt, z, ga, gc, ksum = _inproj(xf, row(g_mix[l]), w_in[l].astype(BF16), row(b_glu[l]), b, s)
        o = _attention(q.reshape(b, s, ATTN_WIDTH), k, vt, ksum.reshape(b, nb, ATTN_WIDTH))
        xf = _mix_ffn(z, w_dw[l], row(b_dw[l]), row(ln_g[l]), row(ln_b[l]),
                      o.reshape(b * s, ATTN_WIDTH), ga, gc, xf,
                      w_attn_up[l].astype(BF16), w_conv_pw[l].astype(BF16), row(b_conv_pw[l]),
                      w_out[l].astype(BF16), row(g_ffn[l]), w_ff1[l].astype(BF16),
                      w_ff2[l].astype(BF16), row(g_final), final_norm=(l == depth - 1), seq=s)
    return xf.reshape(b, s, d)
```

```python
import functools
import math

import jax
import jax.numpy as jnp
from jax import lax
from jax.experimental import pallas as pl
from jax.experimental.pallas import tpu as pltpu

D_MODEL = 1024
N_HEADS = 8
HEAD_DIM = 64
ATTN_WIDTH = N_HEADS * HEAD_DIM
CONV_WIDTH = D_MODEL // 2
CONV_KERNEL = 31
MOBA_BLOCK = 256
MOBA_TOPK = 3
D_FF = 4 * D_MODEL
EPS = 1e-6
IN_COLS = 3 * ATTN_WIDTH + 2 * CONV_WIDTH + 2 * D_MODEL

LANES = 128
SUBLANES = 8
PAIR = LANES // HEAD_DIM
N_PAIRS = N_HEADS // PAIR
K_LANES = 2 * LANES
V_ROWS = HEAD_DIM + 16
Q_GROUP = 4
MASK_BIAS = -(2.0 ** 100)
M_INIT = -(2.0 ** 99)
Q_SCALE = HEAD_DIM ** -0.5 * math.log2(math.e)
GATE_CHUNK = 8
OUT_CHUNK = 4
HALO = 32
CONV_ROWS = 64
TM_INPROJ = 512
TM_FFN = 256
VMEM_LIMIT = 56 * 1024 * 1024

F32 = jnp.float32
BF16 = jnp.bfloat16
NEG_INF = float("-inf")


def _sigmoid(x):
    return 0.5 * jnp.tanh(0.5 * x) + 0.5


def _const_spec(shape):
    return pl.BlockSpec(shape, lambda *_: (0,) * len(shape), pipeline_mode=pl.Buffered(1))


def _inproj_kernel(x_ref, g_ref, w_ref, bglu_ref,
                   q_ref, k_ref, vt_ref, z_ref, ga_ref, gc_ref, ksum_ref):
    x = x_ref[...]
    tm = x.shape[0]
    blocks = tm // MOBA_BLOCK
    ms = jnp.mean(x * x, axis=-1, keepdims=True)
    h = ((x * lax.rsqrt(ms + EPS)) * g_ref[...]).astype(BF16)

    def proj(lo, hi):
        return jnp.dot(h, w_ref[:, lo:hi], preferred_element_type=F32)

    c_q, c_k, c_v = ATTN_WIDTH, 2 * ATTN_WIDTH, 3 * ATTN_WIDTH
    c_a = c_v + CONV_WIDTH
    c_u = c_v + 2 * CONV_WIDTH
    c_ga = c_u + D_MODEL

    q_ref[...] = (proj(0, c_q) * Q_SCALE).astype(BF16)

    k = proj(c_q, c_k)
    sel_lane = lax.broadcasted_iota(jnp.int32, (MOBA_BLOCK, LANES), 1)
    for j in range(blocks):
        kb = k[j * MOBA_BLOCK:(j + 1) * MOBA_BLOCK]
        ksum_ref[0, j] = jnp.sum(kb, axis=0, keepdims=True)
        onehot = (sel_lane == pl.program_id(1) * blocks + j).astype(BF16)
        for p in range(N_PAIRS):
            k_ref[0, p, j, :, 0:LANES] = kb[:, p * LANES:(p + 1) * LANES].astype(BF16)
            k_ref[0, p, j, :, LANES:K_LANES] = onehot

    vt = proj(c_k, c_v).T
    ones_row = (lax.broadcasted_iota(jnp.int32, (V_ROWS - HEAD_DIM, MOBA_BLOCK), 0) == 0).astype(BF16)
    for hd in range(N_HEADS):
        for j in range(blocks):
            vt_ref[0, hd, j, 0:HEAD_DIM, :] = vt[hd * HEAD_DIM:(hd + 1) * HEAD_DIM,
                                                 j * MOBA_BLOCK:(j + 1) * MOBA_BLOCK].astype(BF16)
            vt_ref[0, hd, j, HEAD_DIM:V_ROWS, :] = ones_row

    a = proj(c_v, c_a) + bglu_ref[:, :CONV_WIDTH]
    gg = proj(c_a, c_u) + bglu_ref[:, CONV_WIDTH:]
    z_ref[...] = a * _sigmoid(gg)
    ga_ref[...] = _sigmoid(proj(c_u, c_ga)).astype(BF16)
    gc_ref[...] = _sigmoid(proj(c_ga, IN_COLS)).astype(BF16)


def _inproj(x, g, w, bglu, batch, seq):
    t = x.shape[0]
    tm = TM_INPROJ
    nt = seq // tm
    nb = seq // MOBA_BLOCK
    bpt = tm // MOBA_BLOCK
    row = lambda w_: pl.BlockSpec((tm, w_), lambda b, i: (b * nt + i, 0))
    return pl.pallas_call(
        _inproj_kernel,
        grid=(batch, nt),
        in_specs=[row(D_MODEL), _const_spec((1, D_MODEL)), _const_spec((D_MODEL, IN_COLS)),
                  _const_spec((1, 2 * CONV_WIDTH))],
        out_specs=[
            row(ATTN_WIDTH),
            pl.BlockSpec((1, N_PAIRS, bpt, MOBA_BLOCK, K_LANES), lambda b, i: (b, 0, i, 0, 0)),
            pl.BlockSpec((1, N_HEADS, bpt, V_ROWS, MOBA_BLOCK), lambda b, i: (b, 0, i, 0, 0)),
            row(CONV_WIDTH), row(D_MODEL), row(D_MODEL),
            pl.BlockSpec((1, bpt, 1, ATTN_WIDTH), lambda b, i: (b, i, 0, 0)),
        ],
        out_shape=[
            jax.ShapeDtypeStruct((t, ATTN_WIDTH), BF16),
            jax.ShapeDtypeStruct((batch, N_PAIRS, nb, MOBA_BLOCK, K_LANES), BF16),
            jax.ShapeDtypeStruct((batch, N_HEADS, nb, V_ROWS, MOBA_BLOCK), BF16),
            jax.ShapeDtypeStruct((t, CONV_WIDTH), F32),
            jax.ShapeDtypeStruct((t, D_MODEL), BF16),
            jax.ShapeDtypeStruct((t, D_MODEL), BF16),
            jax.ShapeDtypeStruct((batch, nb, 1, ATTN_WIDTH), F32),
        ],
        compiler_params=pltpu.CompilerParams(dimension_semantics=("arbitrary", "arbitrary"),
                                             vmem_limit_bytes=VMEM_LIMIT),
        name="inproj",
    )(x, g, w, bglu)


def _tile_schedule(n_blocks, slots):
    n_pairs = n_blocks * (n_blocks + 1) // 2
    assert n_pairs % slots == 0
    remaining = {qb: list(range(qb)) for qb in range(1, n_blocks)}
    steps, prev = [], set()
    for s in range(n_pairs // slots):
        step = [(s, s)] if s < n_blocks else []
        blocked = {qb for _, qb in step} | prev | {s + 1}
        assert not prev & {qb for _, qb in step}
        free = sorted((qb for qb, js in remaining.items() if js and qb not in blocked),
                      key=lambda qb: (-len(remaining[qb]), qb))
        step += [(remaining[qb].pop(0), qb) for qb in free[:slots - len(step)]]
        assert len(step) == slots
        steps.append(step)
        prev = {qb for _, qb in step}
    assert not any(remaining.values())
    return steps


def _attn_kernel(kt_ref, qt_ref, q_ref, k_ref, vt_ref, ksum_ref, o_ref, qa_ref, m_ref, acc_ref, s_ref, p_ref,
                 *, n_blocks, n_steps):
    blk_len = MOBA_BLOCK
    sel_rows = slice(LANES, LANES + n_blocks)

    kmean = ksum_ref[0] * (1.0 / blk_len)
    km_hi = kmean.astype(BF16)
    r1 = kmean - km_hi.astype(F32)
    km_mid = r1.astype(BF16)
    km_lo = (r1 - km_mid.astype(F32)).astype(BF16)
    km3 = jnp.concatenate([km_hi, km_mid, km_lo], axis=0)
    width = GATE_CHUNK * blk_len
    row = lax.broadcasted_iota(jnp.int32, (LANES, width), 0)
    blk = lax.broadcasted_iota(jnp.int32, (n_blocks, width), 0)
    col_blk = lax.broadcasted_iota(jnp.int32, (n_blocks, width), 1) // blk_len

    def gate_chunk(c, carry):
        qt = q_ref[0, pl.ds(pl.multiple_of(c * width, width), width), :].astype(F32).T
        q_blk = col_blk + c * GATE_CHUNK
        for hd in range(PAIR):
            in_head = (row >= HEAD_DIM * hd) & (row < HEAD_DIM * (hd + 1))
            qht = jnp.where(in_head, qt, 0.0).astype(BF16)
            g3 = jnp.dot(km3, qht, preferred_element_type=F32)
            gate = g3[0:n_blocks] + g3[n_blocks:2 * n_blocks] + g3[2 * n_blocks:3 * n_blocks]
            g = jnp.where(blk < q_blk, gate, NEG_INF)
            keep = blk == q_blk
            for _ in range(MOBA_TOPK):
                mx = jnp.max(g, axis=0, keepdims=True)
                idx = jnp.min(jnp.where(g == mx, blk, n_blocks), axis=0, keepdims=True)
                pick = blk == idx
                keep = keep | (pick & jnp.isfinite(mx))
                g = jnp.where(pick, NEG_INF, g)
            bias_t = jnp.where(keep, 0.0, MASK_BIAS).astype(BF16)
            for t in range(GATE_CHUNK):
                cols = slice(t * blk_len, (t + 1) * blk_len)
                qa_ref[c * GATE_CHUNK + t, hd, 0:LANES, :] = qht[:, cols]
                qa_ref[c * GATE_CHUNK + t, hd, sel_rows, :] = bias_t[:, cols]
        return carry

    lax.fori_loop(0, n_blocks // GATE_CHUNK, gate_chunk, 0)

    qa_ref[:, :, LANES + n_blocks:K_LANES, :] = jnp.zeros(
        (n_blocks, PAIR, K_LANES - LANES - n_blocks, blk_len), BF16)
    m_ref[...] = jnp.full(m_ref.shape, M_INIT, F32)
    acc_ref[...] = jnp.zeros(acc_ref.shape, F32)

    n_tiles = Q_GROUP * PAIR
    assert n_steps % 2 == 0
    causal_gap = (lax.broadcasted_iota(jnp.int32, (blk_len, blk_len), 0)
                  - lax.broadcasted_iota(jnp.int32, (blk_len, blk_len), 1))

    def pairs_of(i):
        return [(kt_ref[i * Q_GROUP + u], qt_ref[i * Q_GROUP + u]) for u in range(Q_GROUP)]

    def score_step(slot, pairs):
        col_max = []
        for t in range(n_tiles):
            j, qb = pairs[t // PAIR]
            s = jnp.dot(k_ref[0, 0, j], qa_ref[qb, t % PAIR], preferred_element_type=F32)
            if t // PAIR == 0:
                s = jnp.where(causal_gap <= jnp.where(j == qb, 0, blk_len), s, NEG_INF)
            s_ref[slot, t] = s
            col_max.append(jnp.max(s, axis=0, keepdims=True))
        return tuple(col_max)

    def softmax_step(slot, pairs, col_max):
        alphas = []
        for t in range(n_tiles):
            _, qb = pairs[t // PAIR]
            m_old = m_ref[qb, t % PAIR]
            m_new = jnp.maximum(m_old, col_max[t])
            m_ref[qb, t % PAIR] = m_new
            p_ref[slot, t] = jnp.exp2(s_ref[slot, t] - m_new).astype(BF16)
            alphas.append(jnp.exp2(m_old - m_new))
        return tuple(alphas)

    def pv_step(slot, pairs, alphas):
        for t in range(n_tiles):
            j, qb = pairs[t // PAIR]
            acc_ref[qb, t % PAIR] = alphas[t] * acc_ref[qb, t % PAIR] + jnp.dot(
                vt_ref[0, t % PAIR, j], p_ref[slot, t], preferred_element_type=F32)

    def step(slot, i, state):
        col_max, alphas_prev = state
        alphas = softmax_step(slot, pairs_of(i), col_max)
        col_max_next = score_step(1 - slot, pairs_of(i + 1))
        pv_step(1 - slot, pairs_of(jnp.maximum(i - 1, 0)), alphas_prev)
        return col_max_next, alphas

    col_max0 = score_step(0, pairs_of(0))
    p_ref[1] = jnp.zeros(p_ref.shape[1:], BF16)
    start = (col_max0, tuple(jnp.ones((1, blk_len), F32) for _ in range(n_tiles)))
    _, alphas_last = lax.fori_loop(
        0, n_steps // 2, lambda it, st: step(1, 2 * it + 1, step(0, 2 * it, st)), start)
    pv_step(1, pairs_of(n_steps - 1), alphas_last)

    def finalize(c, carry):
        outs = []
        for u in range(OUT_CHUNK):
            for hd in range(PAIR):
                acc = acc_ref[c * OUT_CHUNK + u, hd]
                outs.append(acc[0:HEAD_DIM] / acc[HEAD_DIM:HEAD_DIM + 1])
        width = OUT_CHUNK * blk_len
        rows = pl.ds(pl.multiple_of(c * width, width), width)
        tiles = [jnp.concatenate(outs[u * PAIR:(u + 1) * PAIR], axis=0) for u in range(OUT_CHUNK)]
        o_ref[0, rows, :] = jnp.concatenate(tiles, axis=1).T.astype(BF16)
        return carry

    lax.fori_loop(0, n_blocks // OUT_CHUNK, finalize, 0)


def _attention(q, k, vt, ksum):
    b, s, _ = q.shape
    nb = s // MOBA_BLOCK
    blk = MOBA_BLOCK
    assert LANES + nb <= K_LANES and nb % GATE_CHUNK == 0 and nb % 16 == 0 and nb % OUT_CHUNK == 0
    steps = _tile_schedule(nb, Q_GROUP)
    steps.append(steps[-1])
    key_blocks = jnp.asarray([j for step in steps for j, _ in step], jnp.int32)
    query_blocks = jnp.asarray([qb for step in steps for _, qb in step], jnp.int32)
    smem = pl.BlockSpec(memory_space=pltpu.SMEM)
    return pl.pallas_call(
        functools.partial(_attn_kernel, n_blocks=nb, n_steps=len(steps) - 1),
        grid=(b, N_PAIRS),
        in_specs=[
            smem, smem,
            pl.BlockSpec((1, s, LANES), lambda bi, p: (bi, 0, p)),
            pl.BlockSpec((1, 1, nb, blk, K_LANES), lambda bi, p: (bi, p, 0, 0, 0)),
            pl.BlockSpec((1, PAIR, nb, V_ROWS, blk), lambda bi, p: (bi, p, 0, 0, 0)),
            pl.BlockSpec((1, nb, LANES), lambda bi, p: (bi, 0, p)),
        ],
        out_specs=pl.BlockSpec((1, s, LANES), lambda bi, p: (bi, 0, p)),
        out_shape=jax.ShapeDtypeStruct((b, s, ATTN_WIDTH), BF16),
        scratch_shapes=[pltpu.VMEM((nb, PAIR, K_LANES, blk), BF16),
                        pltpu.VMEM((nb, PAIR, 1, blk), F32),
                        pltpu.VMEM((nb, PAIR, V_ROWS, blk), F32),
                        pltpu.VMEM((2, Q_GROUP * PAIR, blk, blk), F32),
                        pltpu.VMEM((2, Q_GROUP * PAIR, blk, blk), BF16)],
        compiler_params=pltpu.CompilerParams(dimension_semantics=("arbitrary", "arbitrary"),
                                             vmem_limit_bytes=VMEM_LIMIT),
        name="moba_attn",
    )(key_blocks, query_blocks, q, k, vt, ksum)


def _conv_fill(z_ref, halo_ref, first, zbuf):
    tm = z_ref.shape[0]
    zbuf[0:HALO, :] = jnp.where(first, 0.0, halo_ref[...])
    zbuf[HALO:HALO + tm, :] = z_ref[...]
    zbuf[HALO + tm:HALO + tm + SUBLANES, :] = jnp.zeros((SUBLANES, CONV_WIDTH), F32)


def _conv_chunk(r, zbuf, w_ref, bdw_ref, lng_ref, lnb_ref):
    lead = HALO - (CONV_KERNEL - 1)
    base = r * CONV_ROWS
    acc = jnp.zeros((CONV_ROWS, CONV_WIDTH), F32) + bdw_ref[...]
    for phase in range(SUBLANES):
        offsets = [o for o in range(lead, lead + CONV_KERNEL) if o % SUBLANES == phase]
        part = None
        for o in offsets:
            start = base + o - phase
            term = w_ref[o - lead:o - lead + 1, :] * zbuf[start:start + CONV_ROWS + SUBLANES, :]
            part = term if part is None else part + term
        acc = acc + part[phase:phase + CONV_ROWS]
    mu = jnp.mean(acc, axis=-1, keepdims=True)
    d = acc - mu
    var = jnp.mean(d * d, axis=-1, keepdims=True)
    y = (d * lax.rsqrt(var + EPS)) * lng_ref[...] + lnb_ref[...]
    return (y * _sigmoid(y)).astype(BF16)


def _mix_ffn_kernel(z_ref, halo_ref, wdw_ref, bdw_ref, lng_ref, lnb_ref,
                    o_ref, ga_ref, gc_ref, x_ref, wup_ref, wpw_ref, bpw_ref, wout_ref,
                    gffn_ref, w1_ref, w2_ref, gfin_ref, out_ref, t_ref, zbuf, cz_ref,
                    *, final_norm, n_tiles, tiles_per_seq):
    step = pl.program_id(0)
    tm = z_ref.shape[0]

    @pl.when(step == 0)
    def _():
        cz_ref[1] = jnp.zeros(cz_ref.shape[1:], BF16)

    conv_tile = jnp.minimum(step, n_tiles - 1)
    _conv_fill(z_ref, halo_ref, conv_tile % tiles_per_seq == 0, zbuf)
    for r in range(tm // CONV_ROWS):
        cz_ref[step % 2, r * CONV_ROWS:(r + 1) * CONV_ROWS, :] = _conv_chunk(
            r, zbuf, wdw_ref, bdw_ref, lng_ref, lnb_ref)

    def rms(v, g_ref):
        ms = jnp.mean(v * v, axis=-1, keepdims=True)
        return (v * lax.rsqrt(ms + EPS)) * g_ref[...]

    branch_a = jnp.dot(o_ref[...], wup_ref[...], preferred_element_type=F32)
    branch_c = jnp.dot(cz_ref[(step + 1) % 2], wpw_ref[...], preferred_element_type=F32) + bpw_ref[...]
    merged = ga_ref[...].astype(F32) * branch_a + gc_ref[...].astype(F32) * branch_c
    x1 = x_ref[...] + jnp.dot(merged.astype(BF16), wout_ref[...], preferred_element_type=F32)
    h = rms(x1, gffn_ref).astype(BF16)
    chunk = D_MODEL
    for c in range(D_FF // chunk):
        u = jnp.dot(h, w1_ref[:, c * chunk:(c + 1) * chunk], preferred_element_type=F32)
        t_ref[:, c * chunk:(c + 1) * chunk] = jnp.square(jnp.maximum(u, 0.0)).astype(BF16)
    x2 = x1 + jnp.dot(t_ref[...], w2_ref[...], preferred_element_type=F32)
    out_ref[...] = rms(x2, gfin_ref) if final_norm else x2


def _mix_ffn(z, wdw, bdw, lng, lnb, o, ga, gc, x, wup, wpw, bpw, wout, gffn, w1, w2, gfin, final_norm, seq):
    t = x.shape[0]
    tm = TM_FFN
    nt = t // tm
    halo_blocks = tm // HALO
    conv_row = lambda s: jnp.minimum(s, nt - 1)
    row = lambda w_: pl.BlockSpec((tm, w_), lambda s: (jnp.maximum(s - 1, 0), 0))
    return pl.pallas_call(
        functools.partial(_mix_ffn_kernel, final_norm=final_norm, n_tiles=nt, tiles_per_seq=seq // tm),
        grid=(nt + 1,),
        in_specs=[pl.BlockSpec((tm, CONV_WIDTH), lambda s: (conv_row(s), 0)),
                  pl.BlockSpec((HALO, CONV_WIDTH),
                               lambda s: (jnp.maximum(conv_row(s) * halo_blocks - 1, 0), 0)),
                  _const_spec((CONV_KERNEL, CONV_WIDTH)), _const_spec((1, CONV_WIDTH)),
                  _const_spec((1, CONV_WIDTH)), _const_spec((1, CONV_WIDTH)),
                  row(ATTN_WIDTH), row(D_MODEL), row(D_MODEL), row(D_MODEL),
                  _const_spec((ATTN_WIDTH, D_MODEL)), _const_spec((CONV_WIDTH, D_MODEL)),
                  _const_spec((1, D_MODEL)), _const_spec((D_MODEL, D_MODEL)),
                  _const_spec((1, D_MODEL)), _const_spec((D_MODEL, D_FF)),
                  _const_spec((D_FF, D_MODEL)), _const_spec((1, D_MODEL))],
        out_specs=row(D_MODEL),
        out_shape=jax.ShapeDtypeStruct((t, D_MODEL), F32),
        scratch_shapes=[pltpu.VMEM((tm, D_FF), BF16),
                        pltpu.VMEM((HALO + tm + SUBLANES, CONV_WIDTH), F32),
                        pltpu.VMEM((2, tm, CONV_WIDTH), BF16)],
        compiler_params=pltpu.CompilerParams(dimension_semantics=("arbitrary",),
                                             vmem_limit_bytes=VMEM_LIMIT),
        name="mix_ffn",
    )(z, z, wdw, bdw, lng, lnb, o, ga, gc, x, wup, wpw, bpw, wout, gffn, w1, w2, gfin)


def kernel(x, g_mix, w_in, b_glu, w_dw, b_dw, ln_g, ln_b, w_conv_pw, b_conv_pw,
           w_attn_up, w_out, g_ffn, w_ff1, w_ff2, g_final):
    b, s, d = x.shape
    depth = w_in.shape[0]
    nb = s // MOBA_BLOCK
    assert d == D_MODEL and s % TM_INPROJ == 0 and TM_INPROJ % MOBA_BLOCK == 0
    xf = x.reshape(b * s, d)
    row = lambda a: a.reshape(1, -1)
    for l in range(depth):
        q, k, vt, z, ga, gc, ksum = _inproj(xf, row(g_mix[l]), w_in[l].astype(BF16), row(b_glu[l]), b, s)
        o = _attention(q.reshape(b, s, ATTN_WIDTH), k, vt, ksum.reshape(b, nb, ATTN_WIDTH))
        xf = _mix_ffn(z, w_dw[l], row(b_dw[l]), row(ln_g[l]), row(ln_b[l]),
                      o.reshape(b * s, ATTN_WIDTH), ga, gc, xf,
                      w_attn_up[l].astype(BF16), w_conv_pw[l].astype(BF16), row(b_conv_pw[l]),
                      w_out[l].astype(BF16), row(g_ffn[l]), w_ff1[l].astype(BF16),
                      w_ff2[l].astype(BF16), row(g_final), final_norm=(l == depth - 1), seq=s)
    return xf.reshape(b, s, d)
```

```python
import functools
import math

import jax
import jax.numpy as jnp
from jax import lax
from jax.experimental import pallas as pl
from jax.experimental.pallas import tpu as pltpu

D_MODEL = 1024
N_HEADS = 8
HEAD_DIM = 64
ATTN_WIDTH = N_HEADS * HEAD_DIM
CONV_WIDTH = D_MODEL // 2
CONV_KERNEL = 31
MOBA_BLOCK = 256
MOBA_TOPK = 3
D_FF = 4 * D_MODEL
EPS = 1e-6
IN_COLS = 3 * ATTN_WIDTH + 2 * CONV_WIDTH + 2 * D_MODEL

LANES = 128
SUBLANES = 8
PAIR = LANES // HEAD_DIM
N_PAIRS = N_HEADS // PAIR
K_LANES = 2 * LANES
V_ROWS = HEAD_DIM + 16
Q_GROUP = 4
MASK_BIAS = -(2.0 ** 100)
M_INIT = -(2.0 ** 99)
Q_SCALE = HEAD_DIM ** -0.5 * math.log2(math.e)
GATE_CHUNK = 8
OUT_CHUNK = 4
HALO = 32
CONV_ROWS = 64
TM_INPROJ = 1024
TM_FFN = 256
VMEM_LIMIT = 56 * 1024 * 1024

F32 = jnp.float32
BF16 = jnp.bfloat16
NEG_INF = float("-inf")


def _sigmoid(x):
    return 0.5 * jnp.tanh(0.5 * x) + 0.5


def _const_spec(shape):
    return pl.BlockSpec(shape, lambda *_: (0,) * len(shape), pipeline_mode=pl.Buffered(1))


def _layer_spec(shape, layer):
    return pl.BlockSpec((None,) + shape, lambda *_: (layer,) + (0,) * len(shape), pipeline_mode=pl.Buffered(1))


def _inproj_kernel(x_ref, g_ref, w_ref, bglu_ref,
                   q_ref, k_ref, vt_ref, z_ref, ga_ref, gc_ref, ksum_ref):
    x = x_ref[...]
    tm = x.shape[0]
    blocks = tm // MOBA_BLOCK
    ms = jnp.mean(x * x, axis=-1, keepdims=True)
    h = ((x * lax.rsqrt(ms + EPS)) * g_ref[...]).astype(BF16)

    def proj(lo, hi):
        return jnp.dot(h, w_ref[:, lo:hi], preferred_element_type=F32)

    c_q, c_k, c_v = ATTN_WIDTH, 2 * ATTN_WIDTH, 3 * ATTN_WIDTH
    c_a = c_v + CONV_WIDTH
    c_u = c_v + 2 * CONV_WIDTH
    c_ga = c_u + D_MODEL

    ga_ref[...] = _sigmoid(proj(c_u, c_ga)).astype(BF16)
    gc_ref[...] = _sigmoid(proj(c_ga, IN_COLS)).astype(BF16)
    a = proj(c_v, c_a) + bglu_ref[:, :CONV_WIDTH]
    gg = proj(c_a, c_u) + bglu_ref[:, CONV_WIDTH:]
    z_ref[...] = a * _sigmoid(gg)

    k = proj(c_q, c_k)
    sel_lane = lax.broadcasted_iota(jnp.int32, (MOBA_BLOCK, LANES), 1)
    for j in range(blocks):
        kb = k[j * MOBA_BLOCK:(j + 1) * MOBA_BLOCK]
        ksum_ref[0, j] = jnp.sum(kb, axis=0, keepdims=True)
        onehot = (sel_lane == pl.program_id(1) * blocks + j).astype(BF16)
        for p in range(N_PAIRS):
            k_ref[0, p, j, :, 0:LANES] = kb[:, p * LANES:(p + 1) * LANES].astype(BF16)
            k_ref[0, p, j, :, LANES:K_LANES] = onehot

    vt = proj(c_k, c_v).T
    ones_row = (lax.broadcasted_iota(jnp.int32, (V_ROWS - HEAD_DIM, MOBA_BLOCK), 0) == 0).astype(BF16)
    for hd in range(N_HEADS):
        for j in range(blocks):
            vt_ref[0, hd, j, 0:HEAD_DIM, :] = vt[hd * HEAD_DIM:(hd + 1) * HEAD_DIM,
                                                 j * MOBA_BLOCK:(j + 1) * MOBA_BLOCK].astype(BF16)
            vt_ref[0, hd, j, HEAD_DIM:V_ROWS, :] = ones_row

    q_ref[...] = (proj(0, c_q) * Q_SCALE).astype(BF16)


def _inproj(x, g, w, bglu, layer, batch, seq):
    t = x.shape[0]
    tm = TM_INPROJ
    nt = seq // tm
    nb = seq // MOBA_BLOCK
    bpt = tm // MOBA_BLOCK
    row = lambda w_: pl.BlockSpec((tm, w_), lambda b, i: (b * nt + i, 0))
    return pl.pallas_call(
        _inproj_kernel,
        grid=(batch, nt),
        in_specs=[row(D_MODEL), _layer_spec((1, D_MODEL), layer), _layer_spec((D_MODEL, IN_COLS), layer),
                  _layer_spec((1, 2 * CONV_WIDTH), layer)],
        out_specs=[
            row(ATTN_WIDTH),
            pl.BlockSpec((1, N_PAIRS, bpt, MOBA_BLOCK, K_LANES), lambda b, i: (b, 0, i, 0, 0)),
            pl.BlockSpec((1, N_HEADS, bpt, V_ROWS, MOBA_BLOCK), lambda b, i: (b, 0, i, 0, 0)),
            row(CONV_WIDTH), row(D_MODEL), row(D_MODEL),
            pl.BlockSpec((1, bpt, 1, ATTN_WIDTH), lambda b, i: (b, i, 0, 0)),
        ],
        out_shape=[
            jax.ShapeDtypeStruct((t, ATTN_WIDTH), BF16),
            jax.ShapeDtypeStruct((batch, N_PAIRS, nb, MOBA_BLOCK, K_LANES), BF16),
            jax.ShapeDtypeStruct((batch, N_HEADS, nb, V_ROWS, MOBA_BLOCK), BF16),
            jax.ShapeDtypeStruct((t, CONV_WIDTH), F32),
            jax.ShapeDtypeStruct((t, D_MODEL), BF16),
            jax.ShapeDtypeStruct((t, D_MODEL), BF16),
            jax.ShapeDtypeStruct((batch, nb, 1, ATTN_WIDTH), F32),
        ],
        compiler_params=pltpu.CompilerParams(dimension_semantics=("arbitrary", "arbitrary"),
                                             vmem_limit_bytes=VMEM_LIMIT),
        name="inproj",
    )(x, g, w, bglu)


def _tile_schedule(n_blocks, slots):
    n_pairs = n_blocks * (n_blocks + 1) // 2
    assert n_pairs % slots == 0
    remaining = {qb: list(range(qb)) for qb in range(1, n_blocks)}
    steps, prev = [], set()
    for s in range(n_pairs // slots):
        step = [(s, s)] if s < n_blocks else []
        blocked = {qb for _, qb in step} | prev | {s + 1}
        assert not prev & {qb for _, qb in step}
        free = sorted((qb for qb, js in remaining.items() if js and qb not in blocked),
                      key=lambda qb: (-len(remaining[qb]), qb))
        step += [(remaining[qb].pop(0), qb) for qb in free[:slots - len(step)]]
        assert len(step) == slots
        steps.append(step)
        prev = {qb for _, qb in step}
    assert not any(remaining.values())
    return steps


def _attn_kernel(kt_ref, qt_ref, q_ref, k_ref, vt_ref, ksum_ref, o_ref, qa_ref, m_ref, acc_ref, s_ref, p_ref,
                 *, n_blocks, n_steps):
    blk_len = MOBA_BLOCK
    sel_rows = slice(LANES, LANES + n_blocks)

    kmean = ksum_ref[0] * (1.0 / blk_len)
    km_hi = kmean.astype(BF16)
    r1 = kmean - km_hi.astype(F32)
    km_mid = r1.astype(BF16)
    km_lo = (r1 - km_mid.astype(F32)).astype(BF16)
    km3 = jnp.concatenate([km_hi, km_mid, km_lo], axis=0)
    width = GATE_CHUNK * blk_len
    row = lax.broadcasted_iota(jnp.int32, (LANES, width), 0)
    blk = lax.broadcasted_iota(jnp.int32, (n_blocks, width), 0)
    col_blk = lax.broadcasted_iota(jnp.int32, (n_blocks, width), 1) // blk_len

    def gate_chunk(c, carry):
        qt = q_ref[0, pl.ds(pl.multiple_of(c * width, width), width), :].astype(F32).T
        q_blk = col_blk + c * GATE_CHUNK
        for hd in range(PAIR):
            in_head = (row >= HEAD_DIM * hd) & (row < HEAD_DIM * (hd + 1))
            qht = jnp.where(in_head, qt, 0.0).astype(BF16)
            g3 = jnp.dot(km3, qht, preferred_element_type=F32)
            gate = g3[0:n_blocks] + g3[n_blocks:2 * n_blocks] + g3[2 * n_blocks:3 * n_blocks]
            g = jnp.where(blk < q_blk, gate, NEG_INF)
            bias_t = jnp.where(blk == q_blk, 0.0, MASK_BIAS)
            for _ in range(MOBA_TOPK):
                mx = jnp.max(g, axis=0, keepdims=True)
                idx = jnp.min(jnp.where(g == mx, blk, n_blocks), axis=0, keepdims=True)
                pick = blk == idx
                bias_t = jnp.where(pick, jnp.where(jnp.isfinite(mx), 0.0, bias_t), bias_t)
                g = jnp.where(pick, NEG_INF, g)
            bias_t = bias_t.astype(BF16)
            for t in range(GATE_CHUNK):
                cols = slice(t * blk_len, (t + 1) * blk_len)
                qa_ref[c * GATE_CHUNK + t, hd, 0:LANES, :] = qht[:, cols]
                qa_ref[c * GATE_CHUNK + t, hd, sel_rows, :] = bias_t[:, cols]
        return carry

    lax.fori_loop(0, n_blocks // GATE_CHUNK, gate_chunk, 0)

    qa_ref[:, :, LANES + n_blocks:K_LANES, :] = jnp.zeros(
        (n_blocks, PAIR, K_LANES - LANES - n_blocks, blk_len), BF16)
    m_ref[...] = jnp.full(m_ref.shape, M_INIT, F32)
    acc_ref[...] = jnp.zeros(acc_ref.shape, F32)

    n_tiles = Q_GROUP * PAIR
    assert n_steps % 2 == 0
    causal_gap = (lax.broadcasted_iota(jnp.int32, (blk_len, blk_len), 0)
                  - lax.broadcasted_iota(jnp.int32, (blk_len, blk_len), 1))

    def pairs_of(i):
        return [(kt_ref[i * Q_GROUP + u], qt_ref[i * Q_GROUP + u]) for u in range(Q_GROUP)]

    def score_step(slot, pairs):
        col_max = []
        for t in range(n_tiles):
            j, qb = pairs[t // PAIR]
            s = jnp.dot(k_ref[0, 0, j], qa_ref[qb, t % PAIR], preferred_element_type=F32)
            if t // PAIR == 0:
                s = jnp.where(causal_gap <= jnp.where(j == qb, 0, blk_len), s, NEG_INF)
            s_ref[slot, t] = s
            col_max.append(jnp.max(s, axis=0, keepdims=True))
        return tuple(col_max)

    def softmax_step(slot, pairs, col_max):
        alphas = []
        for t in range(n_tiles):
            _, qb = pairs[t // PAIR]
            m_old = m_ref[qb, t % PAIR]
            m_new = jnp.maximum(m_old, col_max[t])
            m_ref[qb, t % PAIR] = m_new
            p_ref[slot, t] = jnp.exp2(s_ref[slot, t] - m_new).astype(BF16)
            alphas.append(jnp.exp2(m_old - m_new))
        return tuple(alphas)

    def pv_step(slot, pairs, alphas):
        for t in range(n_tiles):
            j, qb = pairs[t // PAIR]
            acc_ref[qb, t % PAIR] = alphas[t] * acc_ref[qb, t % PAIR] + jnp.dot(
                vt_ref[0, t % PAIR, j], p_ref[slot, t], preferred_element_type=F32)

    def step(slot, i, state):
        col_max, alphas_prev = state
        alphas = softmax_step(slot, pairs_of(i), col_max)
        col_max_next = score_step(1 - slot, pairs_of(i + 1))
        pv_step(1 - slot, pairs_of(jnp.maximum(i - 1, 0)), alphas_prev)
        return col_max_next, alphas

    col_max0 = score_step(0, pairs_of(0))
    p_ref[1] = jnp.zeros(p_ref.shape[1:], BF16)
    start = (col_max0, tuple(jnp.ones((1, blk_len), F32) for _ in range(n_tiles)))
    _, alphas_last = lax.fori_loop(
        0, n_steps // 2, lambda it, st: step(1, 2 * it + 1, step(0, 2 * it, st)), start)
    pv_step(1, pairs_of(n_steps - 1), alphas_last)

    def finalize(c, carry):
        outs = []
        for u in range(OUT_CHUNK):
            for hd in range(PAIR):
                acc = acc_ref[c * OUT_CHUNK + u, hd]
                outs.append(acc[0:HEAD_DIM] / acc[HEAD_DIM:HEAD_DIM + 1])
        width = OUT_CHUNK * blk_len
        rows = pl.ds(pl.multiple_of(c * width, width), width)
        tiles = [jnp.concatenate(outs[u * PAIR:(u + 1) * PAIR], axis=0) for u in range(OUT_CHUNK)]
        o_ref[0, rows, :] = jnp.concatenate(tiles, axis=1).T.astype(BF16)
        return carry

    lax.fori_loop(0, n_blocks // OUT_CHUNK, finalize, 0)


def _attention(q, k, vt, ksum):
    b, s, _ = q.shape
    nb = s // MOBA_BLOCK
    blk = MOBA_BLOCK
    assert LANES + nb <= K_LANES and nb % GATE_CHUNK == 0 and nb % 16 == 0 and nb % OUT_CHUNK == 0
    steps = _tile_schedule(nb, Q_GROUP)
    steps.append(steps[-1])
    key_blocks = jnp.asarray([j for step in steps for j, _ in step], jnp.int32)
    query_blocks = jnp.asarray([qb for step in steps for _, qb in step], jnp.int32)
    smem = pl.BlockSpec(memory_space=pltpu.SMEM)
    return pl.pallas_call(
        functools.partial(_attn_kernel, n_blocks=nb, n_steps=len(steps) - 1),
        grid=(b, N_PAIRS),
        in_specs=[
            smem, smem,
            pl.BlockSpec((1, s, LANES), lambda bi, p: (bi, 0, p)),
            pl.BlockSpec((1, 1, nb, blk, K_LANES), lambda bi, p: (bi, p, 0, 0, 0)),
            pl.BlockSpec((1, PAIR, nb, V_ROWS, blk), lambda bi, p: (bi, p, 0, 0, 0)),
            pl.BlockSpec((1, nb, LANES), lambda bi, p: (bi, 0, p)),
        ],
        out_specs=pl.BlockSpec((1, s, LANES), lambda bi, p: (bi, 0, p)),
        out_shape=jax.ShapeDtypeStruct((b, s, ATTN_WIDTH), BF16),
        scratch_shapes=[pltpu.VMEM((nb, PAIR, K_LANES, blk), BF16),
                        pltpu.VMEM((nb, PAIR, 1, blk), F32),
                        pltpu.VMEM((nb, PAIR, V_ROWS, blk), F32),
                        pltpu.VMEM((2, Q_GROUP * PAIR, blk, blk), F32),
                        pltpu.VMEM((2, Q_GROUP * PAIR, blk, blk), BF16)],
        compiler_params=pltpu.CompilerParams(dimension_semantics=("arbitrary", "arbitrary"),
                                             vmem_limit_bytes=VMEM_LIMIT),
        name="moba_attn",
    )(key_blocks, query_blocks, q, k, vt, ksum)


def _conv_fill(z_ref, halo_ref, first, zbuf):
    tm = z_ref.shape[0]
    zbuf[0:HALO, :] = jnp.where(first, 0.0, halo_ref[...])
    zbuf[HALO:HALO + tm, :] = z_ref[...]
    zbuf[HALO + tm:HALO + tm + SUBLANES, :] = jnp.zeros((SUBLANES, CONV_WIDTH), F32)


def _conv_chunk(r, zbuf, w_ref, bdw_ref, lng_ref, lnb_ref):
    lead = HALO - (CONV_KERNEL - 1)
    base = r * CONV_ROWS
    acc = jnp.zeros((CONV_ROWS, CONV_WIDTH), F32) + bdw_ref[...]
    for phase in range(SUBLANES):
        offsets = [o for o in range(lead, lead + CONV_KERNEL) if o % SUBLANES == phase]
        part = None
        for o in offsets:
            start = base + o - phase
            term = w_ref[o - lead:o - lead + 1, :] * zbuf[start:start + CONV_ROWS + SUBLANES, :]
            part = term if part is None else part + term
        acc = acc + part[phase:phase + CONV_ROWS]
    mu = jnp.mean(acc, axis=-1, keepdims=True)
    d = acc - mu
    var = jnp.mean(d * d, axis=-1, keepdims=True)
    y = (d * lax.rsqrt(var + EPS)) * lng_ref[...] + lnb_ref[...]
    return (y * _sigmoid(y)).astype(BF16)


def _mix_ffn_kernel(z_ref, halo_ref, wdw_ref, bdw_ref, lng_ref, lnb_ref,
                    o_ref, ga_ref, gc_ref, x_ref, wup_ref, wpw_ref, bpw_ref, wout_ref,
                    gffn_ref, w1_ref, w2_ref, gfin_ref, out_ref, t_ref, zbuf, cz_ref,
                    *, final_norm, n_tiles, tiles_per_seq):
    step = pl.program_id(0)
    tm = z_ref.shape[0]

    @pl.when(step == 0)
    def _():
        cz_ref[1] = jnp.zeros(cz_ref.shape[1:], BF16)

    conv_tile = jnp.minimum(step, n_tiles - 1)
    _conv_fill(z_ref, halo_ref, conv_tile % tiles_per_seq == 0, zbuf)
    for r in range(tm // CONV_ROWS):
        cz_ref[step % 2, r * CONV_ROWS:(r + 1) * CONV_ROWS, :] = _conv_chunk(
            r, zbuf, wdw_ref, bdw_ref, lng_ref, lnb_ref)

    def rms(v, g_ref):
        ms = jnp.mean(v * v, axis=-1, keepdims=True)
        return (v * lax.rsqrt(ms + EPS)) * g_ref[...]

    branch_a = jnp.dot(o_ref[...], wup_ref[...], preferred_element_type=F32)
    branch_c = jnp.dot(cz_ref[(step + 1) % 2], wpw_ref[...], preferred_element_type=F32) + bpw_ref[...]
    merged = ga_ref[...].astype(F32) * branch_a + gc_ref[...].astype(F32) * branch_c
    x1 = x_ref[...] + jnp.dot(merged.astype(BF16), wout_ref[...], preferred_element_type=F32)
    h = rms(x1, gffn_ref).astype(BF16)
    chunk = D_MODEL
    for c in range(D_FF // chunk):
        u = jnp.dot(h, w1_ref[:, c * chunk:(c + 1) * chunk], preferred_element_type=F32)
        t_ref[:, c * chunk:(c + 1) * chunk] = jnp.square(jnp.maximum(u, 0.0)).astype(BF16)
    x2 = x1 + jnp.dot(t_ref[...], w2_ref[...], preferred_element_type=F32)
    out_ref[...] = rms(x2, gfin_ref) if final_norm else x2


def _mix_ffn(z, wdw, bdw, lng, lnb, o, ga, gc, x, wup, wpw, bpw, wout, gffn, w1, w2, gfin,
             layer, final_norm, seq):
    t = x.shape[0]
    tm = TM_FFN
    nt = t // tm
    halo_blocks = tm // HALO
    conv_row = lambda s: jnp.minimum(s, nt - 1)
    row = lambda w_: pl.BlockSpec((tm, w_), lambda s: (jnp.maximum(s - 1, 0), 0))
    per_layer = lambda *shape: _layer_spec(shape, layer)
    return pl.pallas_call(
        functools.partial(_mix_ffn_kernel, final_norm=final_norm, n_tiles=nt, tiles_per_seq=seq // tm),
        grid=(nt + 1,),
        in_specs=[pl.BlockSpec((tm, CONV_WIDTH), lambda s: (conv_row(s), 0)),
                  pl.BlockSpec((HALO, CONV_WIDTH),
                               lambda s: (jnp.maximum(conv_row(s) * halo_blocks - 1, 0), 0)),
                  per_layer(CONV_KERNEL, CONV_WIDTH), per_layer(1, CONV_WIDTH),
                  per_layer(1, CONV_WIDTH), per_layer(1, CONV_WIDTH),
                  row(ATTN_WIDTH), row(D_MODEL), row(D_MODEL), row(D_MODEL),
                  per_layer(ATTN_WIDTH, D_MODEL), per_layer(CONV_WIDTH, D_MODEL),
                  per_layer(1, D_MODEL), per_layer(D_MODEL, D_MODEL),
                  per_layer(1, D_MODEL), per_layer(D_MODEL, D_FF),
                  per_layer(D_FF, D_MODEL), _const_spec((1, D_MODEL))],
        out_specs=row(D_MODEL),
        out_shape=jax.ShapeDtypeStruct((t, D_MODEL), F32),
        scratch_shapes=[pltpu.VMEM((tm, D_FF), BF16),
                        pltpu.VMEM((HALO + tm + SUBLANES, CONV_WIDTH), F32),
                        pltpu.VMEM((2, tm, CONV_WIDTH), BF16)],
        compiler_params=pltpu.CompilerParams(dimension_semantics=("arbitrary",),
                                             vmem_limit_bytes=VMEM_LIMIT),
        name="mix_ffn",
    )(z, z, wdw, bdw, lng, lnb, o, ga, gc, x, wup, wpw, bpw, wout, gffn, w1, w2, gfin)


def kernel(x, g_mix, w_in, b_glu, w_dw, b_dw, ln_g, ln_b, w_conv_pw, b_conv_pw,
           w_attn_up, w_out, g_ffn, w_ff1, w_ff2, g_final):
    b, s, d = x.shape
    depth = w_in.shape[0]
    nb = s // MOBA_BLOCK
    assert d == D_MODEL and s % TM_INPROJ == 0 and TM_INPROJ % MOBA_BLOCK == 0
    xf = x.reshape(b * s, d)
    rows = lambda a: a.reshape(depth, 1, -1)
    w_in, w_attn_up, w_conv_pw, w_out, w_ff1, w_ff2 = (
        w.astype(BF16) for w in (w_in, w_attn_up, w_conv_pw, w_out, w_ff1, w_ff2))
    g_mix, b_glu, b_dw, ln_g, ln_b, b_conv_pw, g_ffn = (
        rows(a) for a in (g_mix, b_glu, b_dw, ln_g, ln_b, b_conv_pw, g_ffn))
    for l in range(depth):
        q, k, vt, z, ga, gc, ksum = _inproj(xf, g_mix, w_in, b_glu, l, b, s)
        o = _attention(q.reshape(b, s, ATTN_WIDTH), k, vt, ksum.reshape(b, nb, ATTN_WIDTH))
        xf = _mix_ffn(z, w_dw, b_dw, ln_g, ln_b, o.reshape(b * s, ATTN_WIDTH), ga, gc, xf,
                      w_attn_up, w_conv_pw, b_conv_pw, w_out, g_ffn, w_ff1, w_ff2, g_final.reshape(1, -1),
                      l, final_norm=(l == depth - 1), seq=s)
    return xf.reshape(b, s, d)
```

```python
import functools
import math

import jax
import jax.numpy as jnp
from jax import lax
from jax.experimental import pallas as pl
from jax.experimental.pallas import tpu as pltpu

D_MODEL = 1024
N_HEADS = 8
HEAD_DIM = 64
ATTN_WIDTH = N_HEADS * HEAD_DIM
CONV_WIDTH = D_MODEL // 2
CONV_KERNEL = 31
MOBA_BLOCK = 256
MOBA_TOPK = 3
D_FF = 4 * D_MODEL
EPS = 1e-6
IN_COLS = 3 * ATTN_WIDTH + 2 * CONV_WIDTH + 2 * D_MODEL

LANES = 128
SUBLANES = 8
PAIR = LANES // HEAD_DIM
N_PAIRS = N_HEADS // PAIR
K_LANES = 2 * LANES
V_ROWS = HEAD_DIM + 16
Q_GROUP = 4
MASK_BIAS = -(2.0 ** 100)
M_INIT = -(2.0 ** 99)
Q_SCALE = HEAD_DIM ** -0.5 * math.log2(math.e)
GATE_CHUNK = 16
OUT_CHUNK = 8
HALO = 32
CONV_ROWS = 64
TM_INPROJ = 1024
TM_FFN = 512
VMEM_LIMIT = 56 * 1024 * 1024

F32 = jnp.float32
BF16 = jnp.bfloat16
NEG_INF = float("-inf")


def _sigmoid(x):
    return 0.5 * jnp.tanh(0.5 * x) + 0.5


def _const_spec(shape):
    return pl.BlockSpec(shape, lambda *_: (0,) * len(shape), pipeline_mode=pl.Buffered(1))


def _layer_spec(shape, layer):
    return pl.BlockSpec((None,) + shape, lambda *_: (layer,) + (0,) * len(shape), pipeline_mode=pl.Buffered(1))


def _inproj_kernel(x_ref, g_ref, w_ref, bglu_ref,
                   q_ref, k_ref, vt_ref, z_ref, ga_ref, gc_ref, ksum_ref):
    x = x_ref[...]
    tm = x.shape[0]
    blocks = tm // MOBA_BLOCK
    ms = jnp.mean(x * x, axis=-1, keepdims=True)
    h = ((x * lax.rsqrt(ms + EPS)) * g_ref[...]).astype(BF16)

    def proj(lo, hi):
        return jnp.dot(h, w_ref[:, lo:hi], preferred_element_type=F32)

    c_q, c_k, c_v = ATTN_WIDTH, 2 * ATTN_WIDTH, 3 * ATTN_WIDTH
    c_a = c_v + CONV_WIDTH
    c_u = c_v + 2 * CONV_WIDTH
    c_ga = c_u + D_MODEL

    ga_ref[...] = _sigmoid(proj(c_u, c_ga)).astype(BF16)
    gc_ref[...] = _sigmoid(proj(c_ga, IN_COLS)).astype(BF16)
    a = proj(c_v, c_a) + bglu_ref[:, :CONV_WIDTH]
    gg = proj(c_a, c_u) + bglu_ref[:, CONV_WIDTH:]
    z_ref[...] = a * _sigmoid(gg)

    k = proj(c_q, c_k)
    sel_lane = lax.broadcasted_iota(jnp.int32, (MOBA_BLOCK, LANES), 1)
    for j in range(blocks):
        kb = k[j * MOBA_BLOCK:(j + 1) * MOBA_BLOCK]
        ksum_ref[0, j] = jnp.sum(kb, axis=0, keepdims=True)
        onehot = (sel_lane == pl.program_id(1) * blocks + j).astype(BF16)
        for p in range(N_PAIRS):
            k_ref[0, p, j, :, 0:LANES] = kb[:, p * LANES:(p + 1) * LANES].astype(BF16)
            k_ref[0, p, j, :, LANES:K_LANES] = onehot

    vt = proj(c_k, c_v).T
    ones_row = (lax.broadcasted_iota(jnp.int32, (V_ROWS - HEAD_DIM, MOBA_BLOCK), 0) == 0).astype(BF16)
    for hd in range(N_HEADS):
        for j in range(blocks):
            vt_ref[0, hd, j, 0:HEAD_DIM, :] = vt[hd * HEAD_DIM:(hd + 1) * HEAD_DIM,
                                                 j * MOBA_BLOCK:(j + 1) * MOBA_BLOCK].astype(BF16)
            vt_ref[0, hd, j, HEAD_DIM:V_ROWS, :] = ones_row

    q_ref[...] = (proj(0, c_q) * Q_SCALE).astype(BF16)


def _inproj(x, g, w, bglu, layer, batch, seq):
    t = x.shape[0]
    tm = TM_INPROJ
    nt = seq // tm
    nb = seq // MOBA_BLOCK
    bpt = tm // MOBA_BLOCK
    row = lambda w_: pl.BlockSpec((tm, w_), lambda b, i: (b * nt + i, 0))
    return pl.pallas_call(
        _inproj_kernel,
        grid=(batch, nt),
        in_specs=[row(D_MODEL), _layer_spec((1, D_MODEL), layer), _layer_spec((D_MODEL, IN_COLS), layer),
                  _layer_spec((1, 2 * CONV_WIDTH), layer)],
        out_specs=[
            row(ATTN_WIDTH),
            pl.BlockSpec((1, N_PAIRS, bpt, MOBA_BLOCK, K_LANES), lambda b, i: (b, 0, i, 0, 0)),
            pl.BlockSpec((1, N_HEADS, bpt, V_ROWS, MOBA_BLOCK), lambda b, i: (b, 0, i, 0, 0)),
            row(CONV_WIDTH), row(D_MODEL), row(D_MODEL),
            pl.BlockSpec((1, bpt, 1, ATTN_WIDTH), lambda b, i: (b, i, 0, 0)),
        ],
        out_shape=[
            jax.ShapeDtypeStruct((t, ATTN_WIDTH), BF16),
            jax.ShapeDtypeStruct((batch, N_PAIRS, nb, MOBA_BLOCK, K_LANES), BF16),
            jax.ShapeDtypeStruct((batch, N_HEADS, nb, V_ROWS, MOBA_BLOCK), BF16),
            jax.ShapeDtypeStruct((t, CONV_WIDTH), F32),
            jax.ShapeDtypeStruct((t, D_MODEL), BF16),
            jax.ShapeDtypeStruct((t, D_MODEL), BF16),
            jax.ShapeDtypeStruct((batch, nb, 1, ATTN_WIDTH), F32),
        ],
        compiler_params=pltpu.CompilerParams(dimension_semantics=("arbitrary", "arbitrary"),
                                             vmem_limit_bytes=VMEM_LIMIT),
        name="inproj",
    )(x, g, w, bglu)


def _tile_schedule(n_blocks, slots):
    n_pairs = n_blocks * (n_blocks + 1) // 2
    assert n_pairs % slots == 0
    remaining = {qb: list(range(qb)) for qb in range(1, n_blocks)}
    steps, prev = [], set()
    for s in range(n_pairs // slots):
        step = [(s, s)] if s < n_blocks else []
        blocked = {qb for _, qb in step} | prev | {s + 1}
        assert not prev & {qb for _, qb in step}
        free = sorted((qb for qb, js in remaining.items() if js and qb not in blocked),
                      key=lambda qb: (-len(remaining[qb]), qb))
        step += [(remaining[qb].pop(0), qb) for qb in free[:slots - len(step)]]
        assert len(step) == slots
        steps.append(step)
        prev = {qb for _, qb in step}
    assert not any(remaining.values())
    return steps


def _attn_kernel(kt_ref, qt_ref, q_ref, k_ref, vt_ref, ksum_ref, o_ref, qa_ref, m_ref, acc_ref, s_ref, p_ref,
                 *, n_blocks, n_steps):
    blk_len = MOBA_BLOCK
    sel_rows = slice(LANES, LANES + n_blocks)

    kmean = ksum_ref[0] * (1.0 / blk_len)
    km_hi = kmean.astype(BF16)
    r1 = kmean - km_hi.astype(F32)
    km_mid = r1.astype(BF16)
    km_lo = (r1 - km_mid.astype(F32)).astype(BF16)
    km3 = jnp.concatenate([km_hi, km_mid, km_lo], axis=0)
    width = GATE_CHUNK * blk_len
    row = lax.broadcasted_iota(jnp.int32, (LANES, width), 0)
    blk = lax.broadcasted_iota(jnp.int32, (n_blocks, width), 0)
    col_blk = lax.broadcasted_iota(jnp.int32, (n_blocks, width), 1) // blk_len

    def gate_chunk(c, carry):
        qt = q_ref[0, pl.ds(pl.multiple_of(c * width, width), width), :].astype(F32).T
        q_blk = col_blk + c * GATE_CHUNK
        for hd in range(PAIR):
            in_head = (row >= HEAD_DIM * hd) & (row < HEAD_DIM * (hd + 1))
            qht = jnp.where(in_head, qt, 0.0).astype(BF16)
            g3 = jnp.dot(km3, qht, preferred_element_type=F32)
            gate = g3[0:n_blocks] + g3[n_blocks:2 * n_blocks] + g3[2 * n_blocks:3 * n_blocks]
            g = jnp.where(blk < q_blk, gate, NEG_INF)
            bias_t = jnp.where(blk == q_blk, 0.0, MASK_BIAS)
            for _ in range(MOBA_TOPK):
                mx = jnp.max(g, axis=0, keepdims=True)
                idx = jnp.min(jnp.where(g == mx, blk, n_blocks), axis=0, keepdims=True)
                pick = blk == idx
                bias_t = jnp.where(pick, jnp.where(jnp.isfinite(mx), 0.0, bias_t), bias_t)
                g = jnp.where(pick, NEG_INF, g)
            bias_t = bias_t.astype(BF16)
            for t in range(GATE_CHUNK):
                cols = slice(t * blk_len, (t + 1) * blk_len)
                qa_ref[c * GATE_CHUNK + t, hd, 0:LANES, :] = qht[:, cols]
                qa_ref[c * GATE_CHUNK + t, hd, sel_rows, :] = bias_t[:, cols]
        return carry

    lax.fori_loop(0, n_blocks // GATE_CHUNK, gate_chunk, 0)

    qa_ref[:, :, LANES + n_blocks:K_LANES, :] = jnp.zeros(
        (n_blocks, PAIR, K_LANES - LANES - n_blocks, blk_len), BF16)
    m_ref[...] = jnp.full(m_ref.shape, M_INIT, F32)
    acc_ref[...] = jnp.zeros(acc_ref.shape, F32)

    n_tiles = Q_GROUP * PAIR
    assert n_steps % 2 == 0
    causal_gap = (lax.broadcasted_iota(jnp.int32, (blk_len, blk_len), 0)
                  - lax.broadcasted_iota(jnp.int32, (blk_len, blk_len), 1))

    def pairs_of(i):
        return [(kt_ref[i * Q_GROUP + u], qt_ref[i * Q_GROUP + u]) for u in range(Q_GROUP)]

    def score_step(slot, pairs):
        col_max = []
        for t in range(n_tiles):
            j, qb = pairs[t // PAIR]
            s = jnp.dot(k_ref[0, 0, j], qa_ref[qb, t % PAIR], preferred_element_type=F32)
            if t // PAIR == 0:
                s = jnp.where(causal_gap <= jnp.where(j == qb, 0, blk_len), s, NEG_INF)
            s_ref[slot, t] = s
            col_max.append(jnp.max(s, axis=0, keepdims=True))
        return tuple(col_max)

    def softmax_step(slot, pairs, col_max):
        alphas = []
        for t in range(n_tiles):
            _, qb = pairs[t // PAIR]
            m_old = m_ref[qb, t % PAIR]
            m_new = jnp.maximum(m_old, col_max[t])
            m_ref[qb, t % PAIR] = m_new
            p_ref[slot, t] = jnp.exp2(s_ref[slot, t] - m_new).astype(BF16)
            alphas.append(jnp.exp2(m_old - m_new))
        return tuple(alphas)

    def pv_step(slot, pairs, alphas):
        for t in range(n_tiles):
            j, qb = pairs[t // PAIR]
            acc_ref[qb, t % PAIR] = alphas[t] * acc_ref[qb, t % PAIR] + jnp.dot(
                vt_ref[0, t % PAIR, j], p_ref[slot, t], preferred_element_type=F32)

    def step(slot, i, state):
        col_max, alphas_prev = state
        alphas = softmax_step(slot, pairs_of(i), col_max)
        col_max_next = score_step(1 - slot, pairs_of(i + 1))
        pv_step(1 - slot, pairs_of(jnp.maximum(i - 1, 0)), alphas_prev)
        return col_max_next, alphas

    col_max0 = score_step(0, pairs_of(0))
    p_ref[1] = jnp.zeros(p_ref.shape[1:], BF16)
    start = (col_max0, tuple(jnp.ones((1, blk_len), F32) for _ in range(n_tiles)))
    _, alphas_last = lax.fori_loop(
        0, n_steps // 2, lambda it, st: step(1, 2 * it + 1, step(0, 2 * it, st)), start)
    pv_step(1, pairs_of(n_steps - 1), alphas_last)

    def finalize(c, carry):
        outs = []
        for u in range(OUT_CHUNK):
            for hd in range(PAIR):
                acc = acc_ref[c * OUT_CHUNK + u, hd]
                outs.append(acc[0:HEAD_DIM] / acc[HEAD_DIM:HEAD_DIM + 1])
        width = OUT_CHUNK * blk_len
        rows = pl.ds(pl.multiple_of(c * width, width), width)
        tiles = [jnp.concatenate(outs[u * PAIR:(u + 1) * PAIR], axis=0) for u in range(OUT_CHUNK)]
        o_ref[0, rows, :] = jnp.concatenate(tiles, axis=1).T.astype(BF16)
        return carry

    lax.fori_loop(0, n_blocks // OUT_CHUNK, finalize, 0)


def _attention(q, k, vt, ksum):
    b, s, _ = q.shape
    nb = s // MOBA_BLOCK
    blk = MOBA_BLOCK
    assert LANES + nb <= K_LANES and nb % GATE_CHUNK == 0 and nb % 16 == 0 and nb % OUT_CHUNK == 0
    steps = _tile_schedule(nb, Q_GROUP)
    steps.append(steps[-1])
    key_blocks = jnp.asarray([j for step in steps for j, _ in step], jnp.int32)
    query_blocks = jnp.asarray([qb for step in steps for _, qb in step], jnp.int32)
    smem = pl.BlockSpec(memory_space=pltpu.SMEM)
    return pl.pallas_call(
        functools.partial(_attn_kernel, n_blocks=nb, n_steps=len(steps) - 1),
        grid=(b, N_PAIRS),
        in_specs=[
            smem, smem,
            pl.BlockSpec((1, s, LANES), lambda bi, p: (bi, 0, p)),
            pl.BlockSpec((1, 1, nb, blk, K_LANES), lambda bi, p: (bi, p, 0, 0, 0)),
            pl.BlockSpec((1, PAIR, nb, V_ROWS, blk), lambda bi, p: (bi, p, 0, 0, 0)),
            pl.BlockSpec((1, nb, LANES), lambda bi, p: (bi, 0, p)),
        ],
        out_specs=pl.BlockSpec((1, s, LANES), lambda bi, p: (bi, 0, p)),
        out_shape=jax.ShapeDtypeStruct((b, s, ATTN_WIDTH), BF16),
        scratch_shapes=[pltpu.VMEM((nb, PAIR, K_LANES, blk), BF16),
                        pltpu.VMEM((nb, PAIR, 1, blk), F32),
                        pltpu.VMEM((nb, PAIR, V_ROWS, blk), F32),
                        pltpu.VMEM((2, Q_GROUP * PAIR, blk, blk), F32),
                        pltpu.VMEM((2, Q_GROUP * PAIR, blk, blk), BF16)],
        compiler_params=pltpu.CompilerParams(dimension_semantics=("arbitrary", "arbitrary"),
                                             vmem_limit_bytes=VMEM_LIMIT),
        name="moba_attn",
    )(key_blocks, query_blocks, q, k, vt, ksum)


def _conv_fill(z_ref, halo_ref, first, zbuf):
    tm = z_ref.shape[0]
    zbuf[0:HALO, :] = jnp.where(first, 0.0, halo_ref[...])
    zbuf[HALO:HALO + tm, :] = z_ref[...]
    zbuf[HALO + tm:HALO + tm + SUBLANES, :] = jnp.zeros((SUBLANES, CONV_WIDTH), F32)


def _conv_chunk(r, zbuf, w_ref, bdw_ref, lng_ref, lnb_ref):
    lead = HALO - (CONV_KERNEL - 1)
    base = r * CONV_ROWS
    acc = jnp.zeros((CONV_ROWS, CONV_WIDTH), F32) + bdw_ref[...]
    for phase in range(SUBLANES):
        offsets = [o for o in range(lead, lead + CONV_KERNEL) if o % SUBLANES == phase]
        part = None
        for o in offsets:
            start = base + o - phase
            term = w_ref[o - lead:o - lead + 1, :] * zbuf[start:start + CONV_ROWS + SUBLANES, :]
            part = term if part is None else part + term
        acc = acc + part[phase:phase + CONV_ROWS]
    mu = jnp.mean(acc, axis=-1, keepdims=True)
    d = acc - mu
    var = jnp.mean(d * d, axis=-1, keepdims=True)
    y = (d * lax.rsqrt(var + EPS)) * lng_ref[...] + lnb_ref[...]
    return (y * _sigmoid(y)).astype(BF16)


def _mix_ffn_kernel(z_ref, halo_ref, wdw_ref, bdw_ref, lng_ref, lnb_ref,
                    o_ref, ga_ref, gc_ref, x_ref, wup_ref, wpw_ref, bpw_ref, wout_ref,
                    gffn_ref, w1_ref, w2_ref, gfin_ref, out_ref, t_ref, zbuf, cz_ref,
                    *, final_norm, n_tiles, tiles_per_seq):
    step = pl.program_id(0)
    tm = z_ref.shape[0]

    @pl.when(step == 0)
    def _():
        cz_ref[1] = jnp.zeros(cz_ref.shape[1:], BF16)

    conv_tile = jnp.minimum(step, n_tiles - 1)
    _conv_fill(z_ref, halo_ref, conv_tile % tiles_per_seq == 0, zbuf)
    for r in range(tm // CONV_ROWS):
        cz_ref[step % 2, r * CONV_ROWS:(r + 1) * CONV_ROWS, :] = _conv_chunk(
            r, zbuf, wdw_ref, bdw_ref, lng_ref, lnb_ref)

    def rms(v, g_ref):
        ms = jnp.mean(v * v, axis=-1, keepdims=True)
        return (v * lax.rsqrt(ms + EPS)) * g_ref[...]

    branch_a = jnp.dot(o_ref[...], wup_ref[...], preferred_element_type=F32)
    branch_c = jnp.dot(cz_ref[(step + 1) % 2], wpw_ref[...], preferred_element_type=F32) + bpw_ref[...]
    merged = ga_ref[...].astype(F32) * branch_a + gc_ref[...].astype(F32) * branch_c
    x1 = x_ref[...] + jnp.dot(merged.astype(BF16), wout_ref[...], preferred_element_type=F32)
    h = rms(x1, gffn_ref).astype(BF16)
    chunk = D_MODEL
    for c in range(D_FF // chunk):
        u = jnp.dot(h, w1_ref[:, c * chunk:(c + 1) * chunk], preferred_element_type=F32)
        t_ref[:, c * chunk:(c + 1) * chunk] = jnp.square(jnp.maximum(u, 0.0)).astype(BF16)
    x2 = x1 + jnp.dot(t_ref[...], w2_ref[...], preferred_element_type=F32)
    out_ref[...] = rms(x2, gfin_ref) if final_norm else x2


def _mix_ffn(z, wdw, bdw, lng, lnb, o, ga, gc, x, wup, wpw, bpw, wout, gffn, w1, w2, gfin,
             layer, final_norm, seq):
    t = x.shape[0]
    tm = TM_FFN
    nt = t // tm
    halo_blocks = tm // HALO
    conv_row = lambda s: jnp.minimum(s, nt - 1)
    row = lambda w_: pl.BlockSpec((tm, w_), lambda s: (jnp.maximum(s - 1, 0), 0))
    per_layer = lambda *shape: _layer_spec(shape, layer)
    return pl.pallas_call(
        functools.partial(_mix_ffn_kernel, final_norm=final_norm, n_tiles=nt, tiles_per_seq=seq // tm),
        grid=(nt + 1,),
        in_specs=[pl.BlockSpec((tm, CONV_WIDTH), lambda s: (conv_row(s), 0)),
                  pl.BlockSpec((HALO, CONV_WIDTH),
                               lambda s: (jnp.maximum(conv_row(s) * halo_blocks - 1, 0), 0)),
                  per_layer(CONV_KERNEL, CONV_WIDTH), per_layer(1, CONV_WIDTH),
                  per_layer(1, CONV_WIDTH), per_layer(1, CONV_WIDTH),
                  row(ATTN_WIDTH), row(D_MODEL), row(D_MODEL), row(D_MODEL),
                  per_layer(ATTN_WIDTH, D_MODEL), per_layer(CONV_WIDTH, D_MODEL),
                  per_layer(1, D_MODEL), per_layer(D_MODEL, D_MODEL),
                  per_layer(1, D_MODEL), per_layer(D_MODEL, D_FF),
                  per_layer(D_FF, D_MODEL), _const_spec((1, D_MODEL))],
        out_specs=row(D_MODEL),
        out_shape=jax.ShapeDtypeStruct((t, D_MODEL), F32),
        scratch_shapes=[pltpu.VMEM((tm, D_FF), BF16),
                        pltpu.VMEM((HALO + tm + SUBLANES, CONV_WIDTH), F32),
                        pltpu.VMEM((2, tm, CONV_WIDTH), BF16)],
        compiler_params=pltpu.CompilerParams(dimension_semantics=("arbitrary",),
                                             vmem_limit_bytes=VMEM_LIMIT),
        name="mix_ffn",
    )(z, z, wdw, bdw, lng, lnb, o, ga, gc, x, wup, wpw, bpw, wout, gffn, w1, w2, gfin)


def kernel(x, g_mix, w_in, b_glu, w_dw, b_dw, ln_g, ln_b, w_conv_pw, b_conv_pw,
           w_attn_up, w_out, g_ffn, w_ff1, w_ff2, g_final):
    b, s, d = x.shape
    depth = w_in.shape[0]
    nb = s // MOBA_BLOCK
    assert d == D_MODEL and s % TM_INPROJ == 0 and TM_INPROJ % MOBA_BLOCK == 0
    xf = x.reshape(b * s, d)
    rows = lambda a: a.reshape(depth, 1, -1)
    w_in, w_attn_up, w_conv_pw, w_out, w_ff1, w_ff2 = (
        w.astype(BF16) for w in (w_in, w_attn_up, w_conv_pw, w_out, w_ff1, w_ff2))
    g_mix, b_glu, b_dw, ln_g, ln_b, b_conv_pw, g_ffn = (
        rows(a) for a in (g_mix, b_glu, b_dw, ln_g, ln_b, b_conv_pw, g_ffn))
    for l in range(depth):
        q, k, vt, z, ga, gc, ksum = _inproj(xf, g_mix, w_in, b_glu, l, b, s)
        o = _attention(q.reshape(b, s, ATTN_WIDTH), k, vt, ksum.reshape(b, nb, ATTN_WIDTH))
        xf = _mix_ffn(z, w_dw, b_dw, ln_g, ln_b, o.reshape(b * s, ATTN_WIDTH), ga, gc, xf,
                      w_attn_up, w_conv_pw, b_conv_pw, w_out, g_ffn, w_ff1, w_ff2, g_final.reshape(1, -1),
                      l, final_norm=(l == depth - 1), seq=s)
    return xf.reshape(b, s, d)
```

```python
import functools
import math

import jax
import jax.numpy as jnp
from jax import lax
from jax.experimental import pallas as pl
from jax.experimental.pallas import tpu as pltpu

D_MODEL = 1024
N_HEADS = 8
HEAD_DIM = 64
ATTN_WIDTH = N_HEADS * HEAD_DIM
CONV_WIDTH = D_MODEL // 2
CONV_KERNEL = 31
MOBA_BLOCK = 256
MOBA_TOPK = 3
D_FF = 4 * D_MODEL
EPS = 1e-6
IN_COLS = 3 * ATTN_WIDTH + 2 * CONV_WIDTH + 2 * D_MODEL

LANES = 128
SUBLANES = 8
PAIR = LANES // HEAD_DIM
N_PAIRS = N_HEADS // PAIR
K_LANES = 2 * LANES
V_ROWS = HEAD_DIM + 16
Q_GROUP = 4
MASK_BIAS = -(2.0 ** 100)
M_INIT = -(2.0 ** 99)
Q_SCALE = HEAD_DIM ** -0.5 * math.log2(math.e)
GATE_CHUNK = 16
OUT_CHUNK = 8
HALO = 32
CONV_ROWS = 128
TM_INPROJ = 1024
TM_FFN = 512
VMEM_LIMIT = 56 * 1024 * 1024

F32 = jnp.float32
BF16 = jnp.bfloat16
NEG_INF = float("-inf")


def _sigmoid(x):
    return 0.5 * jnp.tanh(0.5 * x) + 0.5


def _const_spec(shape):
    return pl.BlockSpec(shape, lambda *_: (0,) * len(shape), pipeline_mode=pl.Buffered(1))


def _layer_spec(shape, layer):
    return pl.BlockSpec((None,) + shape, lambda *_: (layer,) + (0,) * len(shape), pipeline_mode=pl.Buffered(1))


def _inproj_kernel(x_ref, g_ref, w_ref, bglu_ref,
                   q_ref, k_ref, vt_ref, z_ref, ga_ref, gc_ref, ksum_ref):
    x = x_ref[...]
    tm = x.shape[0]
    blocks = tm // MOBA_BLOCK
    ms = jnp.mean(x * x, axis=-1, keepdims=True)
    h = ((x * lax.rsqrt(ms + EPS)) * g_ref[...]).astype(BF16)

    def proj(lo, hi):
        return jnp.dot(h, w_ref[:, lo:hi], preferred_element_type=F32)

    c_q, c_k, c_v = ATTN_WIDTH, 2 * ATTN_WIDTH, 3 * ATTN_WIDTH
    c_a = c_v + CONV_WIDTH
    c_u = c_v + 2 * CONV_WIDTH
    c_ga = c_u + D_MODEL

    ga_ref[...] = _sigmoid(proj(c_u, c_ga)).astype(BF16)
    gc_ref[...] = _sigmoid(proj(c_ga, IN_COLS)).astype(BF16)
    a = proj(c_v, c_a) + bglu_ref[:, :CONV_WIDTH]
    gg = proj(c_a, c_u) + bglu_ref[:, CONV_WIDTH:]
    z_ref[...] = a * _sigmoid(gg)

    k = proj(c_q, c_k)
    sel_lane = lax.broadcasted_iota(jnp.int32, (MOBA_BLOCK, LANES), 1)
    for j in range(blocks):
        kb = k[j * MOBA_BLOCK:(j + 1) * MOBA_BLOCK]
        ksum_ref[0, j] = jnp.sum(kb, axis=0, keepdims=True)
        onehot = (sel_lane == pl.program_id(1) * blocks + j).astype(BF16)
        for p in range(N_PAIRS):
            k_ref[0, p, j, :, 0:LANES] = kb[:, p * LANES:(p + 1) * LANES].astype(BF16)
            k_ref[0, p, j, :, LANES:K_LANES] = onehot

    vt = proj(c_k, c_v).T
    ones_row = (lax.broadcasted_iota(jnp.int32, (V_ROWS - HEAD_DIM, MOBA_BLOCK), 0) == 0).astype(BF16)
    for hd in range(N_HEADS):
        for j in range(blocks):
            vt_ref[0, hd, j, 0:HEAD_DIM, :] = vt[hd * HEAD_DIM:(hd + 1) * HEAD_DIM,
                                                 j * MOBA_BLOCK:(j + 1) * MOBA_BLOCK].astype(BF16)
            vt_ref[0, hd, j, HEAD_DIM:V_ROWS, :] = ones_row

    q_ref[...] = (proj(0, c_q) * Q_SCALE).astype(BF16)


def _inproj(x, g, w, bglu, layer, batch, seq):
    t = x.shape[0]
    tm = TM_INPROJ
    nt = seq // tm
    nb = seq // MOBA_BLOCK
    bpt = tm // MOBA_BLOCK
    row = lambda w_: pl.BlockSpec((tm, w_), lambda b, i: (b * nt + i, 0))
    return pl.pallas_call(
        _inproj_kernel,
        grid=(batch, nt),
        in_specs=[row(D_MODEL), _layer_spec((1, D_MODEL), layer), _layer_spec((D_MODEL, IN_COLS), layer),
                  _layer_spec((1, 2 * CONV_WIDTH), layer)],
        out_specs=[
            row(ATTN_WIDTH),
            pl.BlockSpec((1, N_PAIRS, bpt, MOBA_BLOCK, K_LANES), lambda b, i: (b, 0, i, 0, 0)),
            pl.BlockSpec((1, N_HEADS, bpt, V_ROWS, MOBA_BLOCK), lambda b, i: (b, 0, i, 0, 0)),
            row(CONV_WIDTH), row(D_MODEL), row(D_MODEL),
            pl.BlockSpec((1, bpt, 1, ATTN_WIDTH), lambda b, i: (b, i, 0, 0)),
        ],
        out_shape=[
            jax.ShapeDtypeStruct((t, ATTN_WIDTH), BF16),
            jax.ShapeDtypeStruct((batch, N_PAIRS, nb, MOBA_BLOCK, K_LANES), BF16),
            jax.ShapeDtypeStruct((batch, N_HEADS, nb, V_ROWS, MOBA_BLOCK), BF16),
            jax.ShapeDtypeStruct((t, CONV_WIDTH), F32),
            jax.ShapeDtypeStruct((t, D_MODEL), BF16),
            jax.ShapeDtypeStruct((t, D_MODEL), BF16),
            jax.ShapeDtypeStruct((batch, nb, 1, ATTN_WIDTH), F32),
        ],
        compiler_params=pltpu.CompilerParams(dimension_semantics=("arbitrary", "arbitrary"),
                                             vmem_limit_bytes=VMEM_LIMIT),
        name="inproj",
    )(x, g, w, bglu)


def _tile_schedule(n_blocks, slots):
    n_pairs = n_blocks * (n_blocks + 1) // 2
    assert n_pairs % slots == 0
    remaining = {qb: list(range(qb)) for qb in range(1, n_blocks)}
    steps, prev = [], set()
    for s in range(n_pairs // slots):
        step = [(s, s)] if s < n_blocks else []
        blocked = {qb for _, qb in step} | prev | {s + 1}
        assert not prev & {qb for _, qb in step}
        free = sorted((qb for qb, js in remaining.items() if js and qb not in blocked),
                      key=lambda qb: (-len(remaining[qb]), qb))
        step += [(remaining[qb].pop(0), qb) for qb in free[:slots - len(step)]]
        assert len(step) == slots
        steps.append(step)
        prev = {qb for _, qb in step}
    assert not any(remaining.values())
    return steps


def _attn_kernel(kt_ref, qt_ref, q_ref, k_ref, vt_ref, ksum_ref, o_ref, qa_ref, m_ref, acc_ref, s_ref, p_ref,
                 *, n_blocks, n_steps):
    blk_len = MOBA_BLOCK
    sel_rows = slice(LANES, LANES + n_blocks)

    kmean = ksum_ref[0] * (1.0 / blk_len)
    km_hi = kmean.astype(BF16)
    r1 = kmean - km_hi.astype(F32)
    km_mid = r1.astype(BF16)
    km_lo = (r1 - km_mid.astype(F32)).astype(BF16)
    km3 = jnp.concatenate([km_hi, km_mid, km_lo], axis=0)
    width = GATE_CHUNK * blk_len
    row = lax.broadcasted_iota(jnp.int32, (LANES, width), 0)
    blk = lax.broadcasted_iota(jnp.int32, (n_blocks, width), 0)
    col_blk = lax.broadcasted_iota(jnp.int32, (n_blocks, width), 1) // blk_len

    def gate_chunk(c, carry):
        qt = q_ref[0, pl.ds(pl.multiple_of(c * width, width), width), :].astype(F32).T
        q_blk = col_blk + c * GATE_CHUNK
        for hd in range(PAIR):
            in_head = (row >= HEAD_DIM * hd) & (row < HEAD_DIM * (hd + 1))
            qht = jnp.where(in_head, qt, 0.0).astype(BF16)
            g3 = jnp.dot(km3, qht, preferred_element_type=F32)
            gate = g3[0:n_blocks] + g3[n_blocks:2 * n_blocks] + g3[2 * n_blocks:3 * n_blocks]
            g = jnp.where(blk < q_blk, gate, NEG_INF)
            bias_t = jnp.where(blk == q_blk, 0.0, MASK_BIAS)
            for _ in range(MOBA_TOPK):
                mx = jnp.max(g, axis=0, keepdims=True)
                idx = jnp.min(jnp.where(g == mx, blk, n_blocks), axis=0, keepdims=True)
                pick = blk == idx
                bias_t = jnp.where(pick, jnp.where(jnp.isfinite(mx), 0.0, bias_t), bias_t)
                g = jnp.where(pick, NEG_INF, g)
            bias_t = bias_t.astype(BF16)
            for t in range(GATE_CHUNK):
                cols = slice(t * blk_len, (t + 1) * blk_len)
                qa_ref[c * GATE_CHUNK + t, hd, 0:LANES, :] = qht[:, cols]
                qa_ref[c * GATE_CHUNK + t, hd, sel_rows, :] = bias_t[:, cols]
        return carry

    lax.fori_loop(0, n_blocks // GATE_CHUNK, gate_chunk, 0)

    qa_ref[:, :, LANES + n_blocks:K_LANES, :] = jnp.zeros(
        (n_blocks, PAIR, K_LANES - LANES - n_blocks, blk_len), BF16)
    m_ref[...] = jnp.full(m_ref.shape, M_INIT, F32)
    acc_ref[...] = jnp.zeros(acc_ref.shape, F32)

    n_tiles = Q_GROUP * PAIR
    assert n_steps % 2 == 0
    causal_gap = (lax.broadcasted_iota(jnp.int32, (blk_len, blk_len), 0)
                  - lax.broadcasted_iota(jnp.int32, (blk_len, blk_len), 1))

    def pairs_of(i):
        return [(kt_ref[i * Q_GROUP + u], qt_ref[i * Q_GROUP + u]) for u in range(Q_GROUP)]

    def score_step(slot, pairs):
        col_max = []
        for t in range(n_tiles):
            j, qb = pairs[t // PAIR]
            s = jnp.dot(k_ref[0, 0, j], qa_ref[qb, t % PAIR], preferred_element_type=F32)
            if t // PAIR == 0:
                s = jnp.where(causal_gap <= jnp.where(j == qb, 0, blk_len), s, NEG_INF)
            s_ref[slot, t] = s
            col_max.append(jnp.max(s, axis=0, keepdims=True))
        return tuple(col_max)

    def softmax_step(slot, pairs, col_max):
        alphas = []
        for t in range(n_tiles):
            _, qb = pairs[t // PAIR]
            m_old = m_ref[qb, t % PAIR]
            m_new = jnp.maximum(m_old, col_max[t])
            m_ref[qb, t % PAIR] = m_new
            p_ref[slot, t] = jnp.exp2(s_ref[slot, t] - m_new).astype(BF16)
            alphas.append(jnp.exp2(m_old - m_new))
        return tuple(alphas)

    def pv_step(slot, pairs, alphas):
        for t in range(n_tiles):
            j, qb = pairs[t // PAIR]
            acc_ref[qb, t % PAIR] = alphas[t] * acc_ref[qb, t % PAIR] + jnp.dot(
                vt_ref[0, t % PAIR, j], p_ref[slot, t], preferred_element_type=F32)

    def step(slot, i, state):
        col_max, alphas_prev = state
        alphas = softmax_step(slot, pairs_of(i), col_max)
        col_max_next = score_step(1 - slot, pairs_of(i + 1))
        pv_step(1 - slot, pairs_of(jnp.maximum(i - 1, 0)), alphas_prev)
        return col_max_next, alphas

    col_max0 = score_step(0, pairs_of(0))
    p_ref[1] = jnp.zeros(p_ref.shape[1:], BF16)
    start = (col_max0, tuple(jnp.ones((1, blk_len), F32) for _ in range(n_tiles)))
    _, alphas_last = lax.fori_loop(
        0, n_steps // 2, lambda it, st: step(1, 2 * it + 1, step(0, 2 * it, st)), start)
    pv_step(1, pairs_of(n_steps - 1), alphas_last)

    def finalize(c, carry):
        outs = []
        for u in range(OUT_CHUNK):
            for hd in range(PAIR):
                acc = acc_ref[c * OUT_CHUNK + u, hd]
                outs.append(acc[0:HEAD_DIM] / acc[HEAD_DIM:HEAD_DIM + 1])
        width = OUT_CHUNK * blk_len
        rows = pl.ds(pl.multiple_of(c * width, width), width)
        tiles = [jnp.concatenate(outs[u * PAIR:(u + 1) * PAIR], axis=0) for u in range(OUT_CHUNK)]
        o_ref[0, rows, :] = jnp.concatenate(tiles, axis=1).T.astype(BF16)
        return carry

    lax.fori_loop(0, n_blocks // OUT_CHUNK, finalize, 0)


def _attention(q, k, vt, ksum):
    b, s, _ = q.shape
    nb = s // MOBA_BLOCK
    blk = MOBA_BLOCK
    assert LANES + nb <= K_LANES and nb % GATE_CHUNK == 0 and nb % 16 == 0 and nb % OUT_CHUNK == 0
    steps = _tile_schedule(nb, Q_GROUP)
    steps.append(steps[-1])
    key_blocks = jnp.asarray([j for step in steps for j, _ in step], jnp.int32)
    query_blocks = jnp.asarray([qb for step in steps for _, qb in step], jnp.int32)
    smem = pl.BlockSpec(memory_space=pltpu.SMEM)
    return pl.pallas_call(
        functools.partial(_attn_kernel, n_blocks=nb, n_steps=len(steps) - 1),
        grid=(b, N_PAIRS),
        in_specs=[
            smem, smem,
            pl.BlockSpec((1, s, LANES), lambda bi, p: (bi, 0, p)),
            pl.BlockSpec((1, 1, nb, blk, K_LANES), lambda bi, p: (bi, p, 0, 0, 0)),
            pl.BlockSpec((1, PAIR, nb, V_ROWS, blk), lambda bi, p: (bi, p, 0, 0, 0)),
            pl.BlockSpec((1, nb, LANES), lambda bi, p: (bi, 0, p)),
        ],
        out_specs=pl.BlockSpec((1, s, LANES), lambda bi, p: (bi, 0, p)),
        out_shape=jax.ShapeDtypeStruct((b, s, ATTN_WIDTH), BF16),
        scratch_shapes=[pltpu.VMEM((nb, PAIR, K_LANES, blk), BF16),
                        pltpu.VMEM((nb, PAIR, 1, blk), F32),
                        pltpu.VMEM((nb, PAIR, V_ROWS, blk), F32),
                        pltpu.VMEM((2, Q_GROUP * PAIR, blk, blk), F32),
                        pltpu.VMEM((2, Q_GROUP * PAIR, blk, blk), BF16)],
        compiler_params=pltpu.CompilerParams(dimension_semantics=("arbitrary", "arbitrary"),
                                             vmem_limit_bytes=VMEM_LIMIT),
        name="moba_attn",
    )(key_blocks, query_blocks, q, k, vt, ksum)


def _conv_fill(z_ref, halo_ref, first, zbuf):
    tm = z_ref.shape[0]
    zbuf[0:HALO, :] = jnp.where(first, 0.0, halo_ref[...])
    zbuf[HALO:HALO + tm, :] = z_ref[...]
    zbuf[HALO + tm:HALO + tm + SUBLANES, :] = jnp.zeros((SUBLANES, CONV_WIDTH), F32)


def _conv_chunk(r, zbuf, w_ref, bdw_ref, lng_ref, lnb_ref):
    lead = HALO - (CONV_KERNEL - 1)
    base = r * CONV_ROWS
    acc = jnp.zeros((CONV_ROWS, CONV_WIDTH), F32) + bdw_ref[...]
    for phase in range(SUBLANES):
        offsets = [o for o in range(lead, lead + CONV_KERNEL) if o % SUBLANES == phase]
        part = None
        for o in offsets:
            start = base + o - phase
            term = w_ref[o - lead:o - lead + 1, :] * zbuf[start:start + CONV_ROWS + SUBLANES, :]
            part = term if part is None else part + term
        acc = acc + part[phase:phase + CONV_ROWS]
    mu = jnp.mean(acc, axis=-1, keepdims=True)
    d = acc - mu
    var = jnp.mean(d * d, axis=-1, keepdims=True)
    y = (d * lax.rsqrt(var + EPS)) * lng_ref[...] + lnb_ref[...]
    return (y * _sigmoid(y)).astype(BF16)


def _mix_ffn_kernel(z_ref, halo_ref, wdw_ref, bdw_ref, lng_ref, lnb_ref,
                    o_ref, ga_ref, gc_ref, x_ref, wup_ref, wpw_ref, bpw_ref, wout_ref,
                    gffn_ref, w1_ref, w2_ref, gfin_ref, out_ref, t_ref, zbuf, cz_ref,
                    *, final_norm, n_tiles, tiles_per_seq):
    step = pl.program_id(0)
    tm = z_ref.shape[0]

    @pl.when(step == 0)
    def _():
        cz_ref[1] = jnp.zeros(cz_ref.shape[1:], BF16)

    conv_tile = jnp.minimum(step, n_tiles - 1)
    _conv_fill(z_ref, halo_ref, conv_tile % tiles_per_seq == 0, zbuf)
    for r in range(tm // CONV_ROWS):
        cz_ref[step % 2, r * CONV_ROWS:(r + 1) * CONV_ROWS, :] = _conv_chunk(
            r, zbuf, wdw_ref, bdw_ref, lng_ref, lnb_ref)

    def rms(v, g_ref):
        ms = jnp.mean(v * v, axis=-1, keepdims=True)
        return (v * lax.rsqrt(ms + EPS)) * g_ref[...]

    branch_a = jnp.dot(o_ref[...], wup_ref[...], preferred_element_type=F32)
    branch_c = jnp.dot(cz_ref[(step + 1) % 2], wpw_ref[...], preferred_element_type=F32) + bpw_ref[...]
    merged = ga_ref[...].astype(F32) * branch_a + gc_ref[...].astype(F32) * branch_c
    x1 = x_ref[...] + jnp.dot(merged.astype(BF16), wout_ref[...], preferred_element_type=F32)
    h = rms(x1, gffn_ref).astype(BF16)
    chunk = D_MODEL
    for c in range(D_FF // chunk):
        u = jnp.dot(h, w1_ref[:, c * chunk:(c + 1) * chunk], preferred_element_type=F32)
        t_ref[:, c * chunk:(c + 1) * chunk] = jnp.square(jnp.maximum(u, 0.0)).astype(BF16)
    x2 = x1 + jnp.dot(t_ref[...], w2_ref[...], preferred_element_type=F32)
    out_ref[...] = rms(x2, gfin_ref) if final_norm else x2


def _mix_ffn(z, wdw, bdw, lng, lnb, o, ga, gc, x, wup, wpw, bpw, wout, gffn, w1, w2, gfin,
             layer, final_norm, seq):
    t = x.shape[0]
    tm = TM_FFN
    nt = t // tm
    halo_blocks = tm // HALO
    conv_row = lambda s: jnp.minimum(s, nt - 1)
    row = lambda w_: pl.BlockSpec((tm, w_), lambda s: (jnp.maximum(s - 1, 0), 0))
    per_layer = lambda *shape: _layer_spec(shape, layer)
    return pl.pallas_call(
        functools.partial(_mix_ffn_kernel, final_norm=final_norm, n_tiles=nt, tiles_per_seq=seq // tm),
        grid=(nt + 1,),
        in_specs=[pl.BlockSpec((tm, CONV_WIDTH), lambda s: (conv_row(s), 0)),
                  pl.BlockSpec((HALO, CONV_WIDTH),
                               lambda s: (jnp.maximum(conv_row(s) * halo_blocks - 1, 0), 0)),
                  per_layer(CONV_KERNEL, CONV_WIDTH), per_layer(1, CONV_WIDTH),
                  per_layer(1, CONV_WIDTH), per_layer(1, CONV_WIDTH),
                  row(ATTN_WIDTH), row(D_MODEL), row(D_MODEL), row(D_MODEL),
                  per_layer(ATTN_WIDTH, D_MODEL), per_layer(CONV_WIDTH, D_MODEL),
                  per_layer(1, D_MODEL), per_layer(D_MODEL, D_MODEL),
                  per_layer(1, D_MODEL), per_layer(D_MODEL, D_FF),
                  per_layer(D_FF, D_MODEL), _const_spec((1, D_MODEL))],
        out_specs=row(D_MODEL),
        out_shape=jax.ShapeDtypeStruct((t, D_MODEL), F32),
        scratch_shapes=[pltpu.VMEM((tm, D_FF), BF16),
                        pltpu.VMEM((HALO + tm + SUBLANES, CONV_WIDTH), F32),
                        pltpu.VMEM((2, tm, CONV_WIDTH), BF16)],
        compiler_params=pltpu.CompilerParams(dimension_semantics=("arbitrary",),
                                             vmem_limit_bytes=VMEM_LIMIT),
        name="mix_ffn",
    )(z, z, wdw, bdw, lng, lnb, o, ga, gc, x, wup, wpw, bpw, wout, gffn, w1, w2, gfin)


def kernel(x, g_mix, w_in, b_glu, w_dw, b_dw, ln_g, ln_b, w_conv_pw, b_conv_pw,
           w_attn_up, w_out, g_ffn, w_ff1, w_ff2, g_final):
    b, s, d = x.shape
    depth = w_in.shape[0]
    nb = s // MOBA_BLOCK
    assert d == D_MODEL and s % TM_INPROJ == 0 and TM_INPROJ % MOBA_BLOCK == 0
    xf = x.reshape(b * s, d)
    rows = lambda a: a.reshape(depth, 1, -1)
    w_in, w_attn_up, w_conv_pw, w_out, w_ff1, w_ff2 = (
        w.astype(BF16) for w in (w_in, w_attn_up, w_conv_pw, w_out, w_ff1, w_ff2))
    g_mix, b_glu, b_dw, ln_g, ln_b, b_conv_pw, g_ffn = (
        rows(a) for a in (g_mix, b_glu, b_dw, ln_g, ln_b, b_conv_pw, g_ffn))
    for l in range(depth):
        q, k, vt, z, ga, gc, ksum = _inproj(xf, g_mix, w_in, b_glu, l, b, s)
        o = _attention(q.reshape(b, s, ATTN_WIDTH), k, vt, ksum.reshape(b, nb, ATTN_WIDTH))
        xf = _mix_ffn(z, w_dw, b_dw, ln_g, ln_b, o.reshape(b * s, ATTN_WIDTH), ga, gc, xf,
                      w_attn_up, w_conv_pw, b_conv_pw, w_out, g_ffn, w_ff1, w_ff2, g_final.reshape(1, -1),
                      l, final_norm=(l == depth - 1), seq=s)
    return xf.reshape(b, s, d)
```
